```python
import math
import jax, jax.numpy as jnp
from jax import lax
import numpy as np

D_MODEL = 1024
BATCH = 2
SEQ = 16384
DEPTH = 1
DEC_BATCH = 8
DEC_SEQ = 8192
PAST_LEN = 128

HEAD_DIM = 64
DIFF_HEADS = 4
DIFF_V_DIM = 2 * HEAD_DIM
DIL_HEADS = 8
DIL_PATTERNS = ((128, 1), (512, 4), (2048, 16))
QBLK = 128
A_QK_W = DIFF_HEADS * 2 * HEAD_DIM
A_V_W = DIFF_HEADS * DIFF_V_DIM
B_W = DIL_HEADS * HEAD_DIM
IN_W = 2 * A_QK_W + A_V_W + 3 * B_W
MIX_W = A_V_W + B_W
PEER_HEADS = 8
N_KEYS = 128
N_EXPERTS = N_KEYS * N_KEYS
PEER_HALF = 128
PEER_QDIM = 2 * PEER_HALF
PEER_TOPK = 16
TBLK = 128
EPS = 1e-6

kernel_name = "hymba_diff_dilated_peer_encoder"


def rmsnorm(x, g):
    xf = x.astype(jnp.float32)
    y = xf * lax.rsqrt(jnp.mean(xf * xf, axis=-1, keepdims=True) + EPS)
    return (y * g.astype(jnp.float32)).astype(x.dtype)


def alibi_slopes(n):
    return 2.0 ** (-8.0 * jnp.arange(1, n + 1, dtype=jnp.float32) / n)


def diff_attention(q, k, v, lam):
    b, s = q.shape[0], q.shape[1]
    nb = s // QBLK
    slopes = alibi_slopes(DIFF_HEADS)
    kpos = jnp.arange(s, dtype=jnp.float32)
    scale = HEAD_DIM ** -0.5
    qb = q.reshape(b, nb, QBLK, DIFF_HEADS, 2, HEAD_DIM).swapaxes(0, 1)
    starts = jnp.arange(nb) * QBLK

    def block(args):
        qblk, start = args
        qpos = (start + jnp.arange(QBLK)).astype(jnp.float32)
        bias = -slopes[:, None, None] * jnp.abs(qpos[:, None] - kpos[None, :])
        sc = jnp.einsum('bqhcd,bkhcd->bhcqk', qblk, k).astype(jnp.float32) * scale
        p = jax.nn.softmax(sc + bias[None, :, None], axis=-1)
        a = p[:, :, 0] - lam * p[:, :, 1]
        return jnp.einsum('bhqk,bkhe->bqhe', a.astype(v.dtype), v)

    o = lax.map(block, (qb, starts))
    return o.swapaxes(0, 1).reshape(b, s, DIFF_HEADS, DIFF_V_DIM)


def dilated_attention(q, k, v):
    b, s = q.shape[0], q.shape[1]
    nb = s // QBLK
    slopes = alibi_slopes(DIL_HEADS)
    scale = HEAD_DIM ** -0.5
    qb = q.reshape(b, nb, QBLK, DIL_HEADS, HEAD_DIM).swapaxes(0, 1)
    starts = jnp.arange(nb) * QBLK

    def block(args):
        qblk, start = args
        qpos = start + jnp.arange(QBLK)
        outs, lses = [], []
        for window, dil in DIL_PATTERNS:
            half = window // (2 * dil)
            offs = jnp.arange(-half, half + 1) * dil
            idx = qpos[:, None] + offs[None, :]
            valid = (idx >= 0) & (idx < s)
            idx = jnp.clip(idx, 0, s - 1)
            kg = jnp.take(k, idx, axis=1)
            vg = jnp.take(v, idx, axis=1)
            sc = (jnp.einsum('bqhd,bqjhd->bhqj', qblk, kg).astype(jnp.float32) * scale
                  - slopes[:, None, None] * jnp.abs(offs).astype(jnp.float32)[None, None, :])
            sc = jnp.where(valid[None, None], sc, -jnp.inf)
            lse = jax.nn.logsumexp(sc, axis=-1)
            p = jnp.exp(sc - lse[..., None])
            outs.append(jnp.einsum('bhqj,bqjhd->bqhd', p.astype(v.dtype), vg))
            lses.append(lse)
        wts = jax.nn.softmax(jnp.stack(lses), axis=0)
        wts = wts.transpose(0, 1, 3, 2)[..., None]
        out = jnp.sum(wts * jnp.stack(outs).astype(jnp.float32), axis=0)
        return out.astype(v.dtype)

    o = lax.map(block, (qb, starts))
    return o.swapaxes(0, 1).reshape(b, s, DIL_HEADS, HEAD_DIM)


def peer(x, wq, keys1, keys2, u_tab, v_tab):
    b, s, d = x.shape
    xt = x.reshape((b * s) // TBLK, TBLK, d)

    def block(xb):
        q = (xb @ wq).reshape(TBLK, PEER_HEADS, 2, PEER_HALF)
        s1 = jnp.einsum('thd,kd->thk', q[:, :, 0], keys1).astype(jnp.float32)
        s2 = jnp.einsum('thd,kd->thk', q[:, :, 1], keys2).astype(jnp.float32)
        v1, i1 = lax.top_k(s1, PEER_TOPK)
        v2, i2 = lax.top_k(s2, PEER_TOPK)
        cand = (v1[..., :, None] + v2[..., None, :]).reshape(TBLK, PEER_HEADS, PEER_TOPK * PEER_TOPK)
        cidx = (i1[..., :, None] * N_KEYS + i2[..., None, :]).reshape(TBLK, PEER_HEADS, PEER_TOPK * PEER_TOPK)
        top, pos = lax.top_k(cand, PEER_TOPK)
        eidx = jnp.take_along_axis(cidx, pos, axis=-1)
        g = jax.nn.softmax(top, axis=-1)
        u = jnp.take(u_tab, eidx, axis=0)
        act = jax.nn.gelu(jnp.einsum('td,thkd->thk', xb, u).astype(jnp.float32))
        vv = jnp.take(v_tab, eidx, axis=0)
        return jnp.einsum('thk,thkd->td', (g * act).astype(xb.dtype), vv)

    return lax.map(block, xt).reshape(b, s, d)


def trunk(x, norm1_g, w_in, lambda_q1, lambda_k1, lambda_q2, lambda_k2, subln_g, w_out,
          norm2_g, peer_wq, peer_keys1, peer_keys2, peer_u, peer_v, final_g):
    b, s, _ = x.shape
    for l in range(DEPTH):
        xn = rmsnorm(x, norm1_g[l])
        proj = xn @ w_in[l]
        o0 = 0
        qa = proj[..., o0:o0 + A_QK_W].reshape(b, s, DIFF_HEADS, 2, HEAD_DIM); o0 += A_QK_W
        ka = proj[..., o0:o0 + A_QK_W].reshape(b, s, DIFF_HEADS, 2, HEAD_DIM); o0 += A_QK_W
        va = proj[..., o0:o0 + A_V_W].reshape(b, s, DIFF_HEADS, DIFF_V_DIM); o0 += A_V_W
        qb = proj[..., o0:o0 + B_W].reshape(b, s, DIL_HEADS, HEAD_DIM); o0 += B_W
        kb = proj[..., o0:o0 + B_W].reshape(b, s, DIL_HEADS, HEAD_DIM); o0 += B_W
        vb = proj[..., o0:o0 + B_W].reshape(b, s, DIL_HEADS, HEAD_DIM)
        lambda_init = 0.8 - 0.6 * math.exp(-0.3 * l)
        lam = (jnp.exp(jnp.sum(lambda_q1[l].astype(jnp.float32) * lambda_k1[l].astype(jnp.float32)))
               - jnp.exp(jnp.sum(lambda_q2[l].astype(jnp.float32) * lambda_k2[l].astype(jnp.float32)))
               + lambda_init)
        oa = diff_attention(qa, ka, va, lam)
        oa = rmsnorm(oa, subln_g[l]) * (1.0 - lambda_init)
        ob = dilated_attention(qb, kb, vb)
        mix = jnp.concatenate([oa.reshape(b, s, A_V_W), ob.reshape(b, s, B_W)], axis=-1)
        x = x + mix @ w_out[l]
        x = x + peer(rmsnorm(x, norm2_g[l]), peer_wq[l], peer_keys1[l], peer_keys2[l], peer_u[l], peer_v[l])
    return rmsnorm(x, final_g)


def setup_inputs(seed: int = 0) -> dict:
    key = jax.random.key(seed)
    ks = jax.random.split(key, 20)
    f32 = jnp.float32
    nrm = lambda k, shape, sc: jax.random.normal(k, shape, f32) * sc
    return {
        "x_prompt": nrm(ks[0], (BATCH, SEQ, D_MODEL), 1.0),
        "x_sample": nrm(ks[1], (DEC_BATCH, DEC_SEQ, D_MODEL), 1.0),
        "norm1_g": 1.0 + nrm(ks[2], (DEPTH, D_MODEL), 0.02),
        "w_in": nrm(ks[3], (DEPTH, D_MODEL, IN_W), D_MODEL ** -0.5),
        "lambda_q1": nrm(ks[4], (DEPTH, HEAD_DIM), 0.1),
        "lambda_k1": nrm(ks[5], (DEPTH, HEAD_DIM), 0.1),
        "lambda_q2": nrm(ks[6], (DEPTH, HEAD_DIM), 0.1),
        "lambda_k2": nrm(ks[7], (DEPTH, HEAD_DIM), 0.1),
        "subln_g": 1.0 + nrm(ks[8], (DEPTH, DIFF_V_DIM), 0.02),
        "w_out": nrm(ks[9], (DEPTH, MIX_W, D_MODEL), MIX_W ** -0.5),
        "norm2_g": 1.0 + nrm(ks[10], (DEPTH, D_MODEL), 0.02),
        "peer_wq": nrm(ks[11], (DEPTH, D_MODEL, PEER_HEADS * PEER_QDIM), D_MODEL ** -0.5),
        "peer_keys1": nrm(ks[12], (DEPTH, N_KEYS, PEER_HALF), PEER_HALF ** -0.5),
        "peer_keys2": nrm(ks[13], (DEPTH, N_KEYS, PEER_HALF), PEER_HALF ** -0.5),
        "peer_u": nrm(ks[14], (DEPTH, N_EXPERTS, D_MODEL), D_MODEL ** -0.5),
        "peer_v": nrm(ks[15], (DEPTH, N_EXPERTS, D_MODEL), PEER_HEADS ** -0.5),
        "final_g": 1.0 + nrm(ks[16], (D_MODEL,), 0.02),
    }


def reference(x_prompt, x_sample, norm1_g, w_in, lambda_q1, lambda_k1, lambda_q2, lambda_k2,
              subln_g, w_out, norm2_g, peer_wq, peer_keys1, peer_keys2, peer_u, peer_v, final_g):
    y_prompt = trunk(x_prompt, norm1_g, w_in, lambda_q1, lambda_k1, lambda_q2, lambda_k2, subln_g,
                     w_out, norm2_g, peer_wq, peer_keys1, peer_keys2, peer_u, peer_v, final_g)
    y_sample = trunk(x_sample, norm1_g, w_in, lambda_q1, lambda_k1, lambda_q2, lambda_k2, subln_g,
                     w_out, norm2_g, peer_wq, peer_keys1, peer_keys2, peer_u, peer_v, final_g)
    return (y_prompt, y_sample)
```

```python
import functools
import math

import jax
import jax.numpy as jnp
import numpy as np
from jax import lax
from jax.experimental import pallas as pl
from jax.experimental.pallas import tpu as pltpu

D_MODEL = 1024
HEAD_DIM = 64
DIFF_HEADS = 4
DIFF_V_DIM = 2 * HEAD_DIM
DIL_HEADS = 8
DIL_PATTERNS = ((128, 1), (512, 4), (2048, 16))
A_W = DIFF_HEADS * 2 * HEAD_DIM
B_W = DIL_HEADS * HEAD_DIM
PEER_HEADS = 8
N_KEYS = 128
PEER_HALF = 128
PEER_TOPK = 16
EPS = 1e-6
LAMBDA_INIT = 0.8 - 0.6 * math.exp(-0.3 * 0)

LANES = 128
VMEM_LIMIT = 56 * 1024 * 1024

F32 = jnp.float32
BF16 = jnp.bfloat16
NEG_INF = float("-inf")


def _cparams(sem):
    return pltpu.CompilerParams(dimension_semantics=sem, vmem_limit_bytes=VMEM_LIMIT)


def _rms(x, g):
    return x * lax.rsqrt(jnp.mean(x * x, axis=-1, keepdims=True) + EPS) * g


def _in_proj_kernel(x_ref, g_ref, wqa_t_ref, wva_t_ref, wrest_ref,
                    qa_t_ref, ka_ref, va_t_ref, qb_ref, kb_ref, vb_ref):
    xn = _rms(x_ref[0], g_ref[...]).astype(BF16)
    nt = (((1,), (1,)), ((), ()))
    qa_t_ref[0] = lax.dot_general(wqa_t_ref[...], xn, nt, preferred_element_type=F32).astype(BF16)
    va_t_ref[0] = lax.dot_general(wva_t_ref[...], xn, nt, preferred_element_type=F32).astype(BF16)
    rest = jnp.dot(xn, wrest_ref[...], preferred_element_type=F32)
    ka_ref[0] = rest[:, 0 * A_W:1 * A_W].astype(BF16)
    qb_ref[0] = rest[:, 1 * A_W:2 * A_W].astype(BF16)
    kb_ref[0] = rest[:, 2 * A_W:3 * A_W].astype(BF16)
    vb_ref[0] = rest[:, 3 * A_W:4 * A_W].astype(BF16)


def _in_proj(x, g, w_in, tm=512):
    b, s, d = x.shape
    scale = HEAD_DIM ** -0.5
    wqa_t = (w_in[:, 0:A_W] * scale).T.astype(BF16)
    wva_t = w_in[:, 2 * A_W:3 * A_W].T.astype(BF16)
    wrest = jnp.concatenate(
        [w_in[:, A_W:2 * A_W], w_in[:, 3 * A_W:4 * A_W] * scale, w_in[:, 4 * A_W:]], axis=1).astype(BF16)
    tok = lambda i, j: (i, j, 0)
    tr = lambda i, j: (i, 0, j)
    const = lambda i, j: (0, 0)
    return pl.pallas_call(
        _in_proj_kernel,
        grid=(b, s // tm),
        in_specs=[pl.BlockSpec((1, tm, d), tok),
                  pl.BlockSpec((1, d), const),
                  pl.BlockSpec((A_W, d), const),
                  pl.BlockSpec((A_W, d), const),
                  pl.BlockSpec((d, 4 * A_W), const)],
        out_specs=[pl.BlockSpec((1, A_W, tm), tr),
                   pl.BlockSpec((1, tm, A_W), tok),
                   pl.BlockSpec((1, A_W, tm), tr),
                   pl.BlockSpec((1, tm, B_W), tok),
                   pl.BlockSpec((1, tm, B_W), tok),
                   pl.BlockSpec((1, tm, B_W), tok)],
        out_shape=[jax.ShapeDtypeStruct((b, A_W, s), BF16),
                   jax.ShapeDtypeStruct((b, s, A_W), BF16),
                   jax.ShapeDtypeStruct((b, A_W, s), BF16),
                   jax.ShapeDtypeStruct((b, s, B_W), BF16),
                   jax.ShapeDtypeStruct((b, s, B_W), BF16),
                   jax.ShapeDtypeStruct((b, s, B_W), BF16)],
        compiler_params=_cparams(("parallel", "parallel")),
        name="in_proj",
    )(x, g.reshape(1, d), wqa_t, wva_t, wrest)


def _diff_attn_kernel(slopes_ref, qt_ref, k_ref, vt_ref, rel_ref, lam_ref, sg_ref, o_ref,
                      qs_ref, m_ref, l_ref, acc_ref, *, tq, tk):
    h = pl.program_id(1)
    i = pl.program_id(2)
    j = pl.program_id(3)
    nk = pl.num_programs(3)

    @pl.when(j == 0)
    def _init():
        qt = qt_ref[0]
        zero = jnp.zeros((HEAD_DIM, tq), BF16)
        qs_ref[0:HEAD_DIM, 0:tq] = qt[0:HEAD_DIM]
        qs_ref[0:HEAD_DIM, tq:2 * tq] = zero
        qs_ref[HEAD_DIM:, 0:tq] = zero
        qs_ref[HEAD_DIM:, tq:2 * tq] = qt[HEAD_DIM:]
        m_ref[...] = jnp.full(m_ref.shape, NEG_INF, F32)
        l_ref[...] = jnp.zeros(l_ref.shape, F32)
        acc_ref[...] = jnp.zeros(acc_ref.shape, F32)

    s = jnp.dot(k_ref[0], qs_ref[...], preferred_element_type=F32)
    off = (j * tk - i * tq).astype(F32)
    bias = -slopes_ref[h] * jnp.abs(rel_ref[...] + off)
    s = s + jnp.concatenate([bias, bias], axis=1)
    m_old = m_ref[...]
    m_new = jnp.maximum(m_old, jnp.max(s, axis=0, keepdims=True))
    alpha = jnp.exp(m_old - m_new)
    p = jnp.exp(s - m_new)
    l_ref[...] = alpha * l_ref[...] + jnp.sum(p, axis=0, keepdims=True)
    acc_ref[...] = alpha * acc_ref[...] + jnp.dot(vt_ref[0], p.astype(BF16), preferred_element_type=F32)
    m_ref[...] = m_new

    @pl.when(j == nk - 1)
    def _fin():
        lam_v = lam_ref[...]
        lam = (jnp.exp(jnp.sum(lam_v[0:1] * lam_v[1:2], axis=1, keepdims=True))
               - jnp.exp(jnp.sum(lam_v[2:3] * lam_v[3:4], axis=1, keepdims=True)) + LAMBDA_INIT)
        o = acc_ref[...] / l_ref[...]
        o = o[:, 0:tq] - lam * o[:, tq:2 * tq]
        ms = jnp.mean(o * o, axis=0, keepdims=True)
        o = o * lax.rsqrt(ms + EPS) * sg_ref[...] * (1.0 - LAMBDA_INIT)
        o_ref[0] = o.T.astype(BF16)


def _diff_attn(qa_t, ka, va_t, lam_vecs, subln_g, tq=256, tk=512):
    b, _, s = qa_t.shape
    tq, tk = min(tq, s), min(tk, s)
    slopes = jnp.asarray(2.0 ** (-8.0 * np.arange(1, DIFF_HEADS + 1) / DIFF_HEADS), F32)
    rel = (jnp.arange(tk, dtype=F32)[:, None] - jnp.arange(tq, dtype=F32)[None, :])
    kern = functools.partial(_diff_attn_kernel, tq=tq, tk=tk)
    return pl.pallas_call(
        kern,
        grid=(b, DIFF_HEADS, s // tq, s // tk),
        in_specs=[pl.BlockSpec(memory_space=pltpu.SMEM),
                  pl.BlockSpec((1, DIFF_V_DIM, tq), lambda b_, h, i, j: (b_, h, i)),
                  pl.BlockSpec((1, tk, DIFF_V_DIM), lambda b_, h, i, j: (b_, j, h)),
                  pl.BlockSpec((1, DIFF_V_DIM, tk), lambda b_, h, i, j: (b_, h, j)),
                  pl.BlockSpec((tk, tq), lambda b_, h, i, j: (0, 0)),
                  pl.BlockSpec((4, HEAD_DIM), lambda b_, h, i, j: (0, 0)),
                  pl.BlockSpec((DIFF_V_DIM, 1), lambda b_, h, i, j: (0, 0))],
        out_specs=pl.BlockSpec((1, tq, DIFF_V_DIM), lambda b_, h, i, j: (b_, i, h)),
        out_shape=jax.ShapeDtypeStruct((b, s, A_W), BF16),
        scratch_shapes=[pltpu.VMEM((DIFF_V_DIM, 2 * tq), BF16),
                        pltpu.VMEM((1, 2 * tq), F32),
                        pltpu.VMEM((1, 2 * tq), F32),
                        pltpu.VMEM((DIFF_V_DIM, 2 * tq), F32)],
        compiler_params=_cparams(("parallel", "parallel", "parallel", "arbitrary")),
        name="diff_attn",
    )(slopes, qa_t, ka, va_t, rel, lam_vecs, subln_g.reshape(DIFF_V_DIM, 1))


def _dil_attn_kernel(slopes_ref, q_ref, kp_ref, kc_ref, kn_ref, vp_ref, vc_ref, vn_ref,
                     o_ref, lse_ref, *, dil, half, n_sub, blk):
    hp = pl.program_id(2)
    i = pl.program_id(3)
    q = q_ref[0]
    k = jnp.concatenate([kp_ref[0], kc_ref[0], kn_ref[0]], axis=0)
    v = jnp.concatenate([vp_ref[0], vc_ref[0], vn_ref[0]], axis=0)
    row = lax.broadcasted_iota(jnp.int32, (blk, 3 * blk), 0)
    col = lax.broadcasted_iota(jnp.int32, (blk, 3 * blk), 1)
    joff = col - blk - row
    kidx = (i - 1) * blk + col
    valid = (jnp.abs(joff) <= half) & (kidx >= 0) & (kidx < n_sub)
    dist = jnp.abs(joff).astype(F32) * float(dil)
    lane = lax.broadcasted_iota(jnp.int32, (blk, LANES), 1)
    first = lane < HEAD_DIM
    nt = (((1,), (1,)), ((), ()))
    outs, lses = [], []
    for hh in range(2):
        sel = first if hh == 0 else jnp.logical_not(first)
        qh = jnp.where(sel, q, jnp.zeros_like(q))
        s = lax.dot_general(qh, k, nt, preferred_element_type=F32)
        s = jnp.where(valid, s - slopes_ref[2 * hp + hh] * dist, NEG_INF)
        m = jnp.max(s, axis=1, keepdims=True)
        p = jnp.exp(s - m)
        l = jnp.sum(p, axis=1, keepdims=True)
        o = jnp.dot(p.astype(BF16), v, preferred_element_type=F32) / l
        outs.append(o)
        lses.append(jnp.broadcast_to(m + jnp.log(l), (blk, LANES)))
    o_ref[0] = jnp.where(first, outs[0], outs[1]).astype(BF16)
    lse_ref[0] = jnp.where(first, lses[0], lses[1])


def _dil_attn(qb, kb, vb, window, dil, blk=128):
    b, s, w = qb.shape
    n_sub = s // dil
    nq = n_sub // blk
    half = window // (2 * dil)
    shape2 = (b, n_sub, dil * w)
    q2, k2, v2 = qb.reshape(shape2), kb.reshape(shape2), vb.reshape(shape2)
    slopes = jnp.asarray(2.0 ** (-8.0 * np.arange(1, DIL_HEADS + 1) / DIL_HEADS), F32)
    npair = w // LANES
    cur = lambda b_, r, hp, i: (b_, i, r * npair + hp)
    prv = lambda b_, r, hp, i: (b_, jnp.maximum(i - 1, 0), r * npair + hp)
    nxt = lambda b_, r, hp, i: (b_, jnp.minimum(i + 1, nq - 1), r * npair + hp)
    spec = lambda f: pl.BlockSpec((1, blk, LANES), f)
    kern = functools.partial(_dil_attn_kernel, dil=dil, half=half, n_sub=n_sub, blk=blk)
    o, lse = pl.pallas_call(
        kern,
        grid=(b, dil, npair, nq),
        in_specs=[pl.BlockSpec(memory_space=pltpu.SMEM),
                  spec(cur), spec(prv), spec(cur), spec(nxt), spec(prv), spec(cur), spec(nxt)],
        out_specs=[spec(cur), spec(cur)],
        out_shape=[jax.ShapeDtypeStruct(shape2, BF16), jax.ShapeDtypeStruct(shape2, F32)],
        compiler_params=_cparams(("parallel", "parallel", "parallel", "parallel")),
        name=f"dil_attn_d{dil}",
    )(slopes, q2, k2, k2, k2, v2, v2, v2)
    return o.reshape(b, s, w), lse.reshape(b, s, w)


def _out_proj_kernel(x_ref, oa_ref, o1_ref, o2_ref, o3_ref, l1_ref, l2_ref, l3_ref,
                     wa_ref, wb_ref, g_ref, x1_ref, xn_ref):
    l1, l2, l3 = l1_ref[0], l2_ref[0], l3_ref[0]
    m = jnp.maximum(jnp.maximum(l1, l2), l3)
    e1, e2, e3 = jnp.exp(l1 - m), jnp.exp(l2 - m), jnp.exp(l3 - m)
    ob = (e1 * o1_ref[0].astype(F32) + e2 * o2_ref[0].astype(F32) + e3 * o3_ref[0].astype(F32)) / (e1 + e2 + e3)
    y = (x_ref[0]
         + jnp.dot(oa_ref[0], wa_ref[...], preferred_element_type=F32)
         + jnp.dot(ob.astype(BF16), wb_ref[...], preferred_element_type=F32))
    x1_ref[0] = y
    xn_ref[0] = _rms(y, g_ref[...]).astype(BF16)


def _out_proj(x, oa, obs, lses, w_out, g, tm=512):
    b, s, d = x.shape
    wa = w_out[0:A_W].astype(BF16)
    wb = w_out[A_W:].astype(BF16)
    tok = lambda i, j: (i, j, 0)
    const = lambda i, j: (0, 0)
    half = pl.BlockSpec((1, tm, A_W), tok)
    return pl.pallas_call(
        _out_proj_kernel,
        grid=(b, s // tm),
        in_specs=[pl.BlockSpec((1, tm, d), tok), half, half, half, half, half, half, half,
                  pl.BlockSpec((A_W, d), const), pl.BlockSpec((B_W, d), const), pl.BlockSpec((1, d), const)],
        out_specs=[pl.BlockSpec((1, tm, d), tok), pl.BlockSpec((1, tm, d), tok)],
        out_shape=[jax.ShapeDtypeStruct((b, s, d), F32), jax.ShapeDtypeStruct((b, s, d), BF16)],
        compiler_params=_cparams(("parallel", "parallel")),
        name="out_proj",
    )(x, oa, *obs, *lses, wa, wb, g.reshape(1, d))


def _cand_tables():
    groups = [(0, 16), (1, 8)] + [(a, 8) for a in range(2, 8)]
    a_idx, b_idx = [], []
    for a, n in groups:
        a_idx += [a] * n
        b_idx += list(range(n))
    a_idx += list(range(8, 16))
    b_idx += [0] * 8
    a_idx, b_idx = np.array(a_idx), np.array(b_idx)
    ok = (a_idx + 1) * (b_idx + 1) <= PEER_TOPK
    pos = np.where(ok, a_idx * PEER_TOPK + b_idx, -1)
    return groups, pos.astype(np.int32)


def _top_rows(s, k):
    n = s.shape[0]
    iota = lax.broadcasted_iota(jnp.int32, s.shape, 0)
    vals, idxs = [], []
    for _ in range(k):
        m = jnp.max(s, axis=0, keepdims=True)
        ix = jnp.min(jnp.where(s == m, iota, n), axis=0, keepdims=True)
        vals.append(m)
        idxs.append(ix)
        s = jnp.where(iota == ix, NEG_INF, s)
    return jnp.concatenate(vals, axis=0), jnp.concatenate(idxs, axis=0)


def _peer_select_kernel(xn_ref, wq_t_ref, keys_ref, pos_ref, i1_ref, i2_ref, g_ref,
                        q_ref, v_ref, ix_ref, e_ref, gt_ref, *, groups):
    nt = (((1,), (1,)), ((), ()))
    q_ref[...] = lax.dot_general(wq_t_ref[...], xn_ref[...], nt,
                                 preferred_element_type=F32).astype(BF16)
    tt = xn_ref.shape[0]

    def half_topk(hc, carry):
        qh = q_ref[pl.ds(pl.multiple_of(hc * PEER_HALF, PEER_HALF), PEER_HALF), :]
        s = jnp.dot(keys_ref[hc % 2], qh, preferred_element_type=F32)
        vals, idxs = _top_rows(s, PEER_TOPK)
        v_ref[hc] = vals
        ix_ref[hc] = idxs
        return carry

    lax.fori_loop(0, 2 * PEER_HEADS, half_topk, 0)

    pos_tab = pos_ref[...]
    big = jnp.int32(1 << 30)

    def head_select(h, carry):
        v1, v2 = v_ref[2 * h], v_ref[2 * h + 1]
        i1, i2 = ix_ref[2 * h], ix_ref[2 * h + 1]
        v1g, v2g, i1g, i2g = [], [], [], []
        for a, n in groups:
            v1g.append(jnp.broadcast_to(v1[a:a + 1], (n, tt)))
            i1g.append(jnp.broadcast_to(i1[a:a + 1], (n, tt)))
            v2g.append(v2[0:n])
            i2g.append(i2[0:n])
        v1g.append(v1[8:16]); i1g.append(i1[8:16])
        v2g.append(jnp.broadcast_to(v2[0:1], (8, tt))); i2g.append(jnp.broadcast_to(i2[0:1], (8, tt)))
        cand = jnp.concatenate(v1g, axis=0) + jnp.concatenate(v2g, axis=0)
        eidx = jnp.concatenate(i1g, axis=0) * N_KEYS + jnp.concatenate(i2g, axis=0)
        cand = jnp.where(pos_tab >= 0, cand, NEG_INF)
        key = pos_tab * (N_KEYS * N_KEYS) + eidx
        tops, sel = [], []
        for _ in range(PEER_TOPK):
            m = jnp.max(cand, axis=0, keepdims=True)
            kk = jnp.min(jnp.where(cand == m, key, big), axis=0, keepdims=True)
            tops.append(m)
            sel.append(kk)
            cand = jnp.where(key == kk, NEG_INF, cand)
        top = jnp.concatenate(tops, axis=0)
        ek = jnp.concatenate(sel, axis=0) & (N_KEYS * N_KEYS - 1)
        ex = jnp.exp(top - top[0:1])
        gate = ex / jnp.sum(ex, axis=0, keepdims=True)
        e_ref[h] = ek.astype(F32)
        gt_ref[h] = gate
        return carry

    lax.fori_loop(0, PEER_HEADS, head_select, 0)

    ef = e_ref[...].reshape(PEER_HEADS * PEER_TOPK, tt)
    i1f = jnp.floor(ef * (1.0 / N_KEYS))
    i1_ref[...] = i1f.T
    i2_ref[...] = (ef - i1f * N_KEYS).T
    g_ref[...] = gt_ref[...].reshape(PEER_HEADS * PEER_TOPK, tt).T


def _peer_select(xn, wq, keys1, keys2, tt=256):
    t, d = xn.shape
    groups, pos = _cand_tables()
    pos_tab = jnp.broadcast_to(jnp.asarray(pos)[:, None], (pos.shape[0], tt))
    nslot = PEER_HEADS * PEER_TOPK
    qw = 2 * PEER_HEADS * PEER_HALF
    const = lambda i: (0, 0)
    out = pl.BlockSpec((tt, nslot), lambda i: (i, 0))
    kern = functools.partial(_peer_select_kernel, groups=groups)
    return pl.pallas_call(
        kern,
        grid=(t // tt,),
        in_specs=[pl.BlockSpec((tt, d), lambda i: (i, 0)),
                  pl.BlockSpec((qw, d), const),
                  pl.BlockSpec((2, N_KEYS, PEER_HALF), lambda i: (0, 0, 0)),
                  pl.BlockSpec((pos.shape[0], tt), const)],
        out_specs=[out, out, out],
        out_shape=[jax.ShapeDtypeStruct((t, nslot), F32)] * 3,
        scratch_shapes=[pltpu.VMEM((qw, tt), BF16),
                        pltpu.VMEM((2 * PEER_HEADS, PEER_TOPK, tt), F32),
                        pltpu.VMEM((2 * PEER_HEADS, PEER_TOPK, tt), jnp.int32),
                        pltpu.VMEM((PEER_HEADS, PEER_TOPK, tt), F32),
                        pltpu.VMEM((PEER_HEADS, PEER_TOPK, tt), F32)],
        compiler_params=_cparams(("parallel",)),
        name="peer_select",
    )(xn, wq.T.astype(BF16), jnp.stack([keys1, keys2]).astype(BF16), pos_tab)


W_PITCH = N_KEYS + 8


def _gelu(x):
    c = math.sqrt(2.0 / math.pi)
    return 0.5 * x * (1.0 + jnp.tanh(c * (x + 0.044715 * (x * x * x))))


def _peer_experts_kernel(x1_ref, xn_ref, i1_ref, i2_ref, g_ref, u_ref, v_ref, fg_ref, y_ref,
                         w_ref, acc_ref, *, tt, n1):
    c = pl.program_id(1)
    nc = pl.num_programs(1)
    nt = (((1,), (1,)), ((), ()))

    @pl.when(c == 0)
    def _build_gates():
        sub = lax.broadcasted_iota(jnp.int32, (N_KEYS, N_KEYS), 0).astype(F32)

        def one(t, carry):
            r1 = i1_ref[pl.ds(t, 1), :]
            r2 = i2_ref[pl.ds(t, 1), :]
            gg = g_ref[pl.ds(t, 1), :]
            lt = jnp.where(sub == r1, 1.0, 0.0).astype(BF16)
            rt = jnp.where(sub == r2, gg, 0.0).astype(BF16)
            w_ref[pl.ds(t * W_PITCH, N_KEYS), :] = lax.dot_general(lt, rt, nt, preferred_element_type=F32)
            return carry

        lax.fori_loop(0, tt, one, 0)
        acc_ref[...] = jnp.zeros(acc_ref.shape, F32)

    a = lax.dot_general(xn_ref[...], u_ref[...], nt, preferred_element_type=F32)
    wc = jnp.concatenate(
        [w_ref[pl.ds(c * n1 + r, tt, stride=W_PITCH), :] for r in range(n1)], axis=1)
    z = (_gelu(a) * wc).astype(BF16)
    acc_ref[...] += jnp.dot(z, v_ref[...], preferred_element_type=F32)

    @pl.when(c == nc - 1)
    def _fin():
        y_ref[...] = _rms(x1_ref[...] + acc_ref[...], fg_ref[...])


def _peer_experts(x1, xn, i1, i2, gate, u_tab, v_tab, final_g, tt=256, n1=8):
    t, d = x1.shape
    ne = u_tab.shape[0]
    ec = n1 * N_KEYS
    tok = lambda i, c: (i, 0)
    kern = functools.partial(_peer_experts_kernel, tt=tt, n1=n1)
    return pl.pallas_call(
        kern,
        grid=(t // tt, ne // ec),
        in_specs=[pl.BlockSpec((tt, d), tok), pl.BlockSpec((tt, d), tok),
                  pl.BlockSpec((tt, LANES), tok), pl.BlockSpec((tt, LANES), tok), pl.BlockSpec((tt, LANES), tok),
                  pl.BlockSpec((ec, d), lambda i, c: (c, 0)),
                  pl.BlockSpec((ec, d), lambda i, c: (c, 0)),
                  pl.BlockSpec((1, d), lambda i, c: (0, 0))],
        out_specs=pl.BlockSpec((tt, d), tok),
        out_shape=jax.ShapeDtypeStruct((t, d), F32),
        scratch_shapes=[pltpu.VMEM((tt * W_PITCH, N_KEYS), F32),
                        pltpu.VMEM((tt, d), F32)],
        compiler_params=_cparams(("parallel", "arbitrary")),
        name="peer_experts",
    )(x1, xn, i1, i2, gate, u_tab, v_tab, final_g.reshape(1, d))


def _trunk(x, norm1_g, w_in, lam_vecs, subln_g, w_out, norm2_g, wq, keys1, keys2, u_bf, v_bf, final_g):
    b, s, d = x.shape
    qa_t, ka, va_t, qb, kb, vb = _in_proj(x, norm1_g, w_in)
    oa = _diff_attn(qa_t, ka, va_t, lam_vecs, subln_g)
    obs, lses = [], []
    for window, dil in DIL_PATTERNS:
        o, lse = _dil_attn(qb, kb, vb, window, dil)
        obs.append(o)
        lses.append(lse)
    x1, xn2 = _out_proj(x, oa, obs, lses, w_out, norm2_g)
    x1, xn2 = x1.reshape(b * s, d), xn2.reshape(b * s, d)
    i1, i2, gate = _peer_select(xn2, wq, keys1, keys2)
    y = _peer_experts(x1, xn2, i1, i2, gate, u_bf, v_bf, final_g)
    return y.reshape(b, s, d)


def kernel(x_prompt, x_sample, norm1_g, w_in, lambda_q1, lambda_k1, lambda_q2, lambda_k2, subln_g, w_out,
           norm2_g, peer_wq, peer_keys1, peer_keys2, peer_u, peer_v, final_g):
    lam_vecs = jnp.concatenate([lambda_q1, lambda_k1, lambda_q2, lambda_k2], axis=0).astype(F32)
    u_bf = peer_u[0].astype(BF16)
    v_bf = peer_v[0].astype(BF16)
    args = (norm1_g[0], w_in[0], lam_vecs, subln_g[0], w_out[0], norm2_g[0], peer_wq[0],
            peer_keys1[0], peer_keys2[0], u_bf, v_bf, final_g)
    return (_trunk(x_prompt, *args), _trunk(x_sample, *args))
```

```python
import functools
import math

import jax
import jax.numpy as jnp
import numpy as np
from jax import lax
from jax.experimental import pallas as pl
from jax.experimental.pallas import tpu as pltpu

D_MODEL = 1024
HEAD_DIM = 64
DIFF_HEADS = 4
DIFF_V_DIM = 2 * HEAD_DIM
DIL_HEADS = 8
DIL_PATTERNS = ((128, 1), (512, 4), (2048, 16))
A_W = DIFF_HEADS * 2 * HEAD_DIM
B_W = DIL_HEADS * HEAD_DIM
PEER_HEADS = 8
N_KEYS = 128
PEER_HALF = 128
PEER_TOPK = 16
EPS = 1e-6
LAMBDA_INIT = 0.8 - 0.6 * math.exp(-0.3 * 0)

LOG2E = math.log2(math.e)
ATT_CHUNK = 256
VT_ROWS = DIFF_V_DIM + 16
LANES = 128
VMEM_LIMIT = 56 * 1024 * 1024

F32 = jnp.float32
BF16 = jnp.bfloat16
NEG_INF = float("-inf")


def _cparams(sem):
    return pltpu.CompilerParams(dimension_semantics=sem, vmem_limit_bytes=VMEM_LIMIT)


def _unrolled_loop(lo, hi, body, unroll):
    trips = (hi - lo) // unroll

    def group(g, carry):
        for u in range(unroll):
            body(lo + g * unroll + u, carry)
        return carry

    lax.fori_loop(0, trips, group, 0)
    lax.fori_loop(lo + trips * unroll, hi, body, 0)


def _rms(x, g):
    return x * lax.rsqrt(jnp.mean(x * x, axis=-1, keepdims=True) + EPS) * g


def _in_proj_kernel(x_ref, g_ref, wqa_t_ref, wva_t_ref, wrest_ref,
                    qa_t_ref, ka_ref, va_t_ref, qb_ref, kb_ref, vb_ref, *, tkc):
    xn = _rms(x_ref[0], g_ref[...]).astype(BF16)
    tm = xn.shape[0]
    nt = (((1,), (1,)), ((), ()))
    qa_t_ref[0] = lax.dot_general(wqa_t_ref[...], xn, nt, preferred_element_type=F32).astype(BF16)
    va_t = lax.dot_general(wva_t_ref[...], xn, nt, preferred_element_type=F32).astype(BF16)
    ones = jnp.ones((VT_ROWS - DIFF_V_DIM, tkc), BF16)
    for hd in range(DIFF_HEADS):
        for c in range(tm // tkc):
            va_t_ref[0, hd, c, 0:DIFF_V_DIM, :] = va_t[hd * DIFF_V_DIM:(hd + 1) * DIFF_V_DIM, c * tkc:(c + 1) * tkc]
            va_t_ref[0, hd, c, DIFF_V_DIM:VT_ROWS, :] = ones
    rest = jnp.dot(xn, wrest_ref[...], preferred_element_type=F32)
    ka_ref[0] = rest[:, 0 * A_W:1 * A_W].astype(BF16)
    qb_ref[0] = rest[:, 1 * A_W:2 * A_W].astype(BF16)
    kb_ref[0] = rest[:, 2 * A_W:3 * A_W].astype(BF16)
    vb_ref[0] = rest[:, 3 * A_W:4 * A_W].astype(BF16)


def _in_proj(x, g, w_in, tm=512, tkc=ATT_CHUNK):
    b, s, d = x.shape
    scale = HEAD_DIM ** -0.5
    wqa_t = (w_in[:, 0:A_W] * (scale * LOG2E)).T.astype(BF16)
    wva_t = w_in[:, 2 * A_W:3 * A_W].T.astype(BF16)
    wrest = jnp.concatenate(
        [w_in[:, A_W:2 * A_W], w_in[:, 3 * A_W:4 * A_W] * scale, w_in[:, 4 * A_W:]], axis=1).astype(BF16)
    tok = lambda i, j: (i, j, 0)
    tr = lambda i, j: (i, 0, j)
    const = lambda i, j: (0, 0)
    return pl.pallas_call(
        functools.partial(_in_proj_kernel, tkc=tkc),
        grid=(b, s // tm),
        in_specs=[pl.BlockSpec((1, tm, d), tok),
                  pl.BlockSpec((1, d), const),
                  pl.BlockSpec((A_W, d), const),
                  pl.BlockSpec((A_W, d), const),
                  pl.BlockSpec((d, 4 * A_W), const)],
        out_specs=[pl.BlockSpec((1, A_W, tm), tr),
                   pl.BlockSpec((1, tm, A_W), tok),
                   pl.BlockSpec((1, DIFF_HEADS, tm // tkc, VT_ROWS, tkc), lambda i, j: (i, 0, j, 0, 0)),
                   pl.BlockSpec((1, tm, B_W), tok),
                   pl.BlockSpec((1, tm, B_W), tok),
                   pl.BlockSpec((1, tm, B_W), tok)],
        out_shape=[jax.ShapeDtypeStruct((b, A_W, s), BF16),
                   jax.ShapeDtypeStruct((b, s, A_W), BF16),
                   jax.ShapeDtypeStruct((b, DIFF_HEADS, s // tkc, VT_ROWS, tkc), BF16),
                   jax.ShapeDtypeStruct((b, s, B_W), BF16),
                   jax.ShapeDtypeStruct((b, s, B_W), BF16),
                   jax.ShapeDtypeStruct((b, s, B_W), BF16)],
        compiler_params=_cparams(("parallel", "parallel")),
        name="in_proj",
    )(x, g.reshape(1, d), wqa_t, wva_t, wrest)


def _bf16_split3(x):
    x = np.asarray(x, np.float32)
    pieces = []
    for _ in range(3):
        p = x.astype(jnp.bfloat16).astype(np.float32)
        pieces.append(p.astype(np.float64))
        x = (x - p).astype(np.float32)
    return pieces


def _alibi_tables(tq, tkc):
    slopes = 2.0 ** (-8.0 * np.arange(1, DIFF_HEADS + 1) / DIFF_HEADS)
    a = (slopes * LOG2E).astype(np.float32)
    kx = np.zeros((tkc, LANES), np.float32)
    s_rel = np.arange(tkc, dtype=np.float32)
    kx[:, 0:3] = s_rel[:, None]
    kx[:, 3:6] = 1.0
    qx = np.zeros((DIFF_HEADS, 2, LANES, tq), np.float64)
    t_rel = np.arange(tq, dtype=np.float32)
    for hd in range(DIFF_HEADS):
        a3 = _bf16_split3(a[hd])
        f3 = _bf16_split3(-(a[hd] * t_rel).astype(np.float32))
        for r in range(3):
            qx[hd, 0, r, :] = a3[r]
            qx[hd, 0, 3 + r, :] = f3[r]
    qx[:, 1] = -qx[:, 0]
    rel = s_rel[:, None] - t_rel[None, :]
    ndiag = tq // tkc
    diag = np.stack([np.stack([-(a[hd] * np.abs(rel + d * tkc)) for d in range(ndiag)])
                     for hd in range(DIFF_HEADS)]).astype(np.float32)
    return (jnp.asarray(a), jnp.asarray(kx, F32).astype(BF16), jnp.asarray(qx, F32).astype(BF16),
            jnp.asarray(diag))


def _diff_attn_kernel(a_ref, qt_ref, k_ref, vt_ref, kx_ref, qx_ref, diag_ref, lam_ref, sg_ref, o_ref,
                      qs_ref, m_ref, acc_ref, s_ref, *, tq, tkc, unroll):
    h = pl.program_id(1)
    i = pl.program_id(2)
    nk = k_ref.shape[1] // tkc
    ndiag = tq // tkc
    a_h = a_ref[h]

    qt = qt_ref[0]
    zero = jnp.zeros((HEAD_DIM, tq), BF16)
    for side in range(2):
        qs_ref[side, 0:HEAD_DIM, 0:tq] = qt[0:HEAD_DIM]
        qs_ref[side, 0:HEAD_DIM, tq:2 * tq] = zero
        qs_ref[side, HEAD_DIM:DIFF_V_DIM, 0:tq] = zero
        qs_ref[side, HEAD_DIM:DIFF_V_DIM, tq:2 * tq] = qt[HEAD_DIM:]
        qs_ref[side, DIFF_V_DIM:, 0:tq] = qx_ref[0, side]
        qs_ref[side, DIFF_V_DIM:, tq:2 * tq] = qx_ref[0, side]
    m_ref[...] = jnp.full(m_ref.shape, NEG_INF, F32)
    acc_ref[...] = jnp.zeros(acc_ref.shape, F32)

    def update(j, s, c):
        m_old = m_ref[...]
        m_new = jnp.maximum(m_old, jnp.max(s, axis=0, keepdims=True) + c)
        alpha = jnp.exp2(m_old - m_new)
        p = jnp.exp2(s - (m_new - c)).astype(BF16)
        acc_ref[...] = alpha * acc_ref[...] + jnp.dot(vt_ref[0, 0, j], p, preferred_element_type=F32)
        m_ref[...] = m_new

    n_left = i * ndiag
    for d in range(ndiag):
        j = n_left + d
        k0 = pl.multiple_of(j * tkc, tkc)
        s = jnp.dot(k_ref[0, pl.ds(k0, tkc), :], qs_ref[0, 0:DIFF_V_DIM, :], preferred_element_type=F32)
        bias = diag_ref[0, d]
        update(j, s + jnp.concatenate([bias, bias], axis=1), jnp.float32(0.0))

    cnt = nk - ndiag
    last = cnt - 1

    def chunk_of(n):
        n = jnp.minimum(n, last)
        side = (n >= n_left).astype(jnp.int32)
        j = n + side * ndiag
        dist = (1 - 2 * side) * (i * tq - j * tkc)
        return j, side, -a_h * dist.astype(F32)

    def scores(n, slot):
        j, side, _ = chunk_of(n)
        k0 = pl.multiple_of(j * tkc, tkc)
        lhs = jnp.concatenate([k_ref[0, pl.ds(k0, tkc), :], kx_ref[...]], axis=1)
        s_ref[slot] = jnp.dot(lhs, qs_ref[side], preferred_element_type=F32)

    def consume(n, slot):
        j, _, c = chunk_of(n)
        update(j, s_ref[slot], c)

    scores(0, 0)

    def group(g, carry):
        for u in range(unroll):
            scores(g * unroll + u + 1, (u + 1) % 2)
            consume(g * unroll + u, u % 2)
        return carry

    trips = cnt // unroll
    lax.fori_loop(0, trips, group, 0)
    for n in range(trips * unroll, cnt):
        if n + 1 < cnt:
            scores(n + 1, (n + 1) % 2)
        consume(n, n % 2)

    lam_v = lam_ref[...]
    lam = (jnp.exp(jnp.sum(lam_v[0:1] * lam_v[1:2], axis=1, keepdims=True))
           - jnp.exp(jnp.sum(lam_v[2:3] * lam_v[3:4], axis=1, keepdims=True)) + LAMBDA_INIT)
    acc = acc_ref[...]
    o = acc[0:DIFF_V_DIM] / acc[DIFF_V_DIM:DIFF_V_DIM + 1]
    o = o[:, 0:tq] - lam * o[:, tq:2 * tq]
    ms = jnp.mean(o * o, axis=0, keepdims=True)
    o = o * lax.rsqrt(ms + EPS) * sg_ref[...] * (1.0 - LAMBDA_INIT)
    o_ref[0] = o.T.astype(BF16)


def _diff_attn(qa_t, ka, va_t, lam_vecs, subln_g, tq=512, tkc=ATT_CHUNK, unroll=4):
    b, _, s = qa_t.shape
    nk = s // tkc
    a, kx, qx, diag = _alibi_tables(tq, tkc)
    kern = functools.partial(_diff_attn_kernel, tq=tq, tkc=tkc, unroll=unroll)
    c2 = lambda b_, h, i: (0, 0)
    return pl.pallas_call(
        kern,
        grid=(b, DIFF_HEADS, s // tq),
        in_specs=[pl.BlockSpec(memory_space=pltpu.SMEM),
                  pl.BlockSpec((1, DIFF_V_DIM, tq), lambda b_, h, i: (b_, h, i)),
                  pl.BlockSpec((1, s, DIFF_V_DIM), lambda b_, h, i: (b_, 0, h)),
                  pl.BlockSpec((1, 1, nk, VT_ROWS, tkc), lambda b_, h, i: (b_, h, 0, 0, 0)),
                  pl.BlockSpec((tkc, LANES), c2),
                  pl.BlockSpec((1, 2, LANES, tq), lambda b_, h, i: (h, 0, 0, 0)),
                  pl.BlockSpec((1, tq // tkc, tkc, tq), lambda b_, h, i: (h, 0, 0, 0)),
                  pl.BlockSpec((4, HEAD_DIM), c2),
                  pl.BlockSpec((DIFF_V_DIM, 1), c2)],
        out_specs=pl.BlockSpec((1, tq, DIFF_V_DIM), lambda b_, h, i: (b_, i, h)),
        out_shape=jax.ShapeDtypeStruct((b, s, A_W), BF16),
        scratch_shapes=[pltpu.VMEM((2, 2 * LANES, 2 * tq), BF16),
                        pltpu.VMEM((1, 2 * tq), F32),
                        pltpu.VMEM((VT_ROWS, 2 * tq), F32),
                        pltpu.VMEM((2, tkc, 2 * tq), F32)],
        compiler_params=_cparams(("parallel", "parallel", "arbitrary")),
        name="diff_attn",
    )(a, qa_t, ka, va_t, kx, qx, diag, lam_vecs, subln_g.reshape(DIFF_V_DIM, 1))


def _dil_attn_kernel(bias_ref, q_ref, kp_ref, kc_ref, kn_ref, vp_ref, vc_ref, vn_ref,
                     o_ref, lse_ref, *, n_sub, blk, qt):
    i = pl.program_id(2)
    nsb = qt // blk
    col = lax.broadcasted_iota(jnp.int32, (1, 3 * blk), 1)
    lane = lax.broadcasted_iota(jnp.int32, (blk, LANES), 1)
    first = lane < HEAD_DIM
    nt = (((1,), (1,)), ((), ()))
    for sb in range(nsb):
        def window(prev_ref, cur_ref, next_ref, cols):
            parts = [prev_ref[0, :, cols] if sb == 0 else cur_ref[0, (sb - 1) * blk:sb * blk, cols],
                     cur_ref[0, sb * blk:(sb + 1) * blk, cols],
                     next_ref[0, :, cols] if sb == nsb - 1 else cur_ref[0, (sb + 1) * blk:(sb + 2) * blk, cols]]
            return jnp.concatenate(parts, axis=0)

        edge = None
        if sb == 0 or sb == nsb - 1:
            kidx = i * qt + (sb - 1) * blk + col
            edge = (kidx >= 0) & (kidx < n_sub)
        for hp in range(DIL_HEADS // 2):
            cols = slice(hp * LANES, (hp + 1) * LANES)
            q = q_ref[0, sb * blk:(sb + 1) * blk, cols]
            k = window(kp_ref, kc_ref, kn_ref, cols)
            v = window(vp_ref, vc_ref, vn_ref, cols)
            outs, lses = [], []
            for hh in range(2):
                sel = first if hh == 0 else jnp.logical_not(first)
                qh = jnp.where(sel, q, jnp.zeros_like(q))
                s = lax.dot_general(qh, k, nt, preferred_element_type=F32) + bias_ref[2 * hp + hh]
                if edge is not None:
                    s = jnp.where(edge, s, NEG_INF)
                m = jnp.max(s, axis=1, keepdims=True)
                p = jnp.exp(s - m)
                l = jnp.sum(p, axis=1, keepdims=True)
                o = jnp.dot(p.astype(BF16), v, preferred_element_type=F32) / l
                outs.append(o)
                lses.append(jnp.broadcast_to(m + jnp.log(l), (blk, LANES)))
            o_ref[0, sb * blk:(sb + 1) * blk, cols] = jnp.where(first, outs[0], outs[1]).astype(BF16)
            lse_ref[0, sb * blk:(sb + 1) * blk, cols] = jnp.where(first, lses[0], lses[1])


def _dil_attn(qb, kb, vb, window, dil, blk=128, qt=512):
    b, s, w = qb.shape
    n_sub = s // dil
    qt = min(qt, n_sub)
    nq = n_sub // qt
    nblk = n_sub // blk
    half = window // (2 * dil)
    shape2 = (b, n_sub, dil * w)
    q2, k2, v2 = qb.reshape(shape2), kb.reshape(shape2), vb.reshape(shape2)
    slopes = 2.0 ** (-8.0 * np.arange(1, DIL_HEADS + 1) / DIL_HEADS)
    joff = np.arange(3 * blk)[None, :] - blk - np.arange(blk)[:, None]
    band = np.abs(joff) <= half
    bias = np.where(band[None], -slopes[:, None, None] * (np.abs(joff) * dil)[None], -np.inf).astype(np.float32)
    cur = lambda b_, r, i: (b_, i, r)
    prv = lambda b_, r, i: (b_, jnp.maximum(i * (qt // blk) - 1, 0), r)
    nxt = lambda b_, r, i: (b_, jnp.minimum((i + 1) * (qt // blk), nblk - 1), r)
    big = lambda f: pl.BlockSpec((1, qt, w), f)
    halo = lambda f: pl.BlockSpec((1, blk, w), f)
    kern = functools.partial(_dil_attn_kernel, n_sub=n_sub, blk=blk, qt=qt)
    o, lse = pl.pallas_call(
        kern,
        grid=(b, dil, nq),
        in_specs=[pl.BlockSpec((DIL_HEADS, blk, 3 * blk), lambda b_, r, i: (0, 0, 0)),
                  big(cur), halo(prv), big(cur), halo(nxt), halo(prv), big(cur), halo(nxt)],
        out_specs=[big(cur), big(cur)],
        out_shape=[jax.ShapeDtypeStruct(shape2, BF16), jax.ShapeDtypeStruct(shape2, F32)],
        compiler_params=_cparams(("parallel", "parallel", "parallel")),
        name=f"dil_attn_d{dil}",
    )(jnp.asarray(bias), q2, k2, k2, k2, v2, v2, v2)
    return o.reshape(b, s, w), lse.reshape(b, s, w)


def _out_proj_kernel(x_ref, oa_ref, o1_ref, o2_ref, o3_ref, l1_ref, l2_ref, l3_ref,
                     wa_ref, wb_ref, g_ref, x1_ref, xn_ref):
    l1, l2, l3 = l1_ref[0], l2_ref[0], l3_ref[0]
    m = jnp.maximum(jnp.maximum(l1, l2), l3)
    e1, e2, e3 = jnp.exp(l1 - m), jnp.exp(l2 - m), jnp.exp(l3 - m)
    ob = (e1 * o1_ref[0].astype(F32) + e2 * o2_ref[0].astype(F32) + e3 * o3_ref[0].astype(F32)) / (e1 + e2 + e3)
    y = (x_ref[0]
         + jnp.dot(oa_ref[0], wa_ref[...], preferred_element_type=F32)
         + jnp.dot(ob.astype(BF16), wb_ref[...], preferred_element_type=F32))
    x1_ref[0] = y
    xn_ref[0] = _rms(y, g_ref[...]).astype(BF16)


def _out_proj(x, oa, obs, lses, w_out, g, tm=512):
    b, s, d = x.shape
    wa = w_out[0:A_W].astype(BF16)
    wb = w_out[A_W:].astype(BF16)
    tok = lambda i, j: (i, j, 0)
    const = lambda i, j: (0, 0)
    half = pl.BlockSpec((1, tm, A_W), tok)
    return pl.pallas_call(
        _out_proj_kernel,
        grid=(b, s // tm),
        in_specs=[pl.BlockSpec((1, tm, d), tok), half, half, half, half, half, half, half,
                  pl.BlockSpec((A_W, d), const), pl.BlockSpec((B_W, d), const), pl.BlockSpec((1, d), const)],
        out_specs=[pl.BlockSpec((1, tm, d), tok), pl.BlockSpec((1, tm, d), tok)],
        out_shape=[jax.ShapeDtypeStruct((b, s, d), F32), jax.ShapeDtypeStruct((b, s, d), BF16)],
        compiler_params=_cparams(("parallel", "parallel")),
        name="out_proj",
    )(x, oa, *obs, *lses, wa, wb, g.reshape(1, d))


def _cand_tables():
    groups = [(0, 16), (1, 8)] + [(a, 8) for a in range(2, 8)]
    a_idx, b_idx = [], []
    for a, n in groups:
        a_idx += [a] * n
        b_idx += list(range(n))
    a_idx += list(range(8, 16))
    b_idx += [0] * 8
    a_idx, b_idx = np.array(a_idx), np.array(b_idx)
    ok = (a_idx + 1) * (b_idx + 1) <= PEER_TOPK
    pos = np.where(ok, a_idx * PEER_TOPK + b_idx, -1)
    return groups, pos.astype(np.int32)


def _top_rows(s, k):
    n = s.shape[0]
    iota = lax.broadcasted_iota(jnp.int32, s.shape, 0)
    vals, idxs = [], []
    for _ in range(k):
        m = jnp.max(s, axis=0, keepdims=True)
        ix = jnp.min(jnp.where(s == m, iota, n), axis=0, keepdims=True)
        vals.append(m)
        idxs.append(ix)
        s = jnp.where(iota == ix, NEG_INF, s)
    return jnp.concatenate(vals, axis=0), jnp.concatenate(idxs, axis=0)


def _peer_select_kernel(xn_ref, wq_t_ref, keys_ref, pos_ref, i1_ref, i2_ref, g_ref,
                        q_ref, v_ref, ix_ref, e_ref, gt_ref, *, groups):
    nt = (((1,), (1,)), ((), ()))
    q_ref[...] = lax.dot_general(wq_t_ref[...], xn_ref[...], nt,
                                 preferred_element_type=F32).astype(BF16)
    tt = xn_ref.shape[0]

    def half_topk(hc, carry):
        qh = q_ref[pl.ds(pl.multiple_of(hc * PEER_HALF, PEER_HALF), PEER_HALF), :]
        s = jnp.dot(keys_ref[hc % 2], qh, preferred_element_type=F32)
        vals, idxs = _top_rows(s, PEER_TOPK)
        v_ref[hc] = vals
        ix_ref[hc] = idxs
        return carry

    lax.fori_loop(0, 2 * PEER_HEADS, half_topk, 0)

    pos_tab = pos_ref[...]
    big = jnp.int32(1 << 30)

    def head_select(h, carry):
        v1, v2 = v_ref[2 * h], v_ref[2 * h + 1]
        i1, i2 = ix_ref[2 * h], ix_ref[2 * h + 1]
        v1g, v2g, i1g, i2g = [], [], [], []
        for a, n in groups:
            v1g.append(jnp.broadcast_to(v1[a:a + 1], (n, tt)))
            i1g.append(jnp.broadcast_to(i1[a:a + 1], (n, tt)))
            v2g.append(v2[0:n])
            i2g.append(i2[0:n])
        v1g.append(v1[8:16]); i1g.append(i1[8:16])
        v2g.append(jnp.broadcast_to(v2[0:1], (8, tt))); i2g.append(jnp.broadcast_to(i2[0:1], (8, tt)))
        cand = jnp.concatenate(v1g, axis=0) + jnp.concatenate(v2g, axis=0)
        eidx = jnp.concatenate(i1g, axis=0) * N_KEYS + jnp.concatenate(i2g, axis=0)
        cand = jnp.where(pos_tab >= 0, cand, NEG_INF)
        key = pos_tab * (N_KEYS * N_KEYS) + eidx
        tops, sel = [], []
        for _ in range(PEER_TOPK):
            m = jnp.max(cand, axis=0, keepdims=True)
            kk = jnp.min(jnp.where(cand == m, key, big), axis=0, keepdims=True)
            tops.append(m)
            sel.append(kk)
            cand = jnp.where(key == kk, NEG_INF, cand)
        top = jnp.concatenate(tops, axis=0)
        ek = jnp.concatenate(sel, axis=0) & (N_KEYS * N_KEYS - 1)
        ex = jnp.exp(top - top[0:1])
        gate = ex / jnp.sum(ex, axis=0, keepdims=True)
        e_ref[h] = ek.astype(F32)
        gt_ref[h] = gate
        return carry

    lax.fori_loop(0, PEER_HEADS, head_select, 0)

    ef = e_ref[...].reshape(PEER_HEADS * PEER_TOPK, tt)
    i1f = jnp.floor(ef * (1.0 / N_KEYS))
    i1_ref[...] = i1f.T
    i2_ref[...] = (ef - i1f * N_KEYS).T
    g_ref[...] = gt_ref[...].reshape(PEER_HEADS * PEER_TOPK, tt).T


def _peer_select(xn, wq, keys1, keys2, tt=256):
    t, d = xn.shape
    groups, pos = _cand_tables()
    pos_tab = jnp.broadcast_to(jnp.asarray(pos)[:, None], (pos.shape[0], tt))
    nslot = PEER_HEADS * PEER_TOPK
    qw = 2 * PEER_HEADS * PEER_HALF
    const = lambda i: (0, 0)
    out = pl.BlockSpec((tt, nslot), lambda i: (i, 0))
    kern = functools.partial(_peer_select_kernel, groups=groups)
    return pl.pallas_call(
        kern,
        grid=(t // tt,),
        in_specs=[pl.BlockSpec((tt, d), lambda i: (i, 0)),
                  pl.BlockSpec((qw, d), const),
                  pl.BlockSpec((2, N_KEYS, PEER_HALF), lambda i: (0, 0, 0)),
                  pl.BlockSpec((pos.shape[0], tt), const)],
        out_specs=[out, out, out],
        out_shape=[jax.ShapeDtypeStruct((t, nslot), F32)] * 3,
        scratch_shapes=[pltpu.VMEM((qw, tt), BF16),
                        pltpu.VMEM((2 * PEER_HEADS, PEER_TOPK, tt), F32),
                        pltpu.VMEM((2 * PEER_HEADS, PEER_TOPK, tt), jnp.int32),
                        pltpu.VMEM((PEER_HEADS, PEER_TOPK, tt), F32),
                        pltpu.VMEM((PEER_HEADS, PEER_TOPK, tt), F32)],
        compiler_params=_cparams(("parallel",)),
        name="peer_select",
    )(xn, wq.T.astype(BF16), jnp.stack([keys1, keys2]).astype(BF16), pos_tab)


W_ROWS = N_KEYS // 2
W_PITCH = W_ROWS + 8
I1_GROUP = 8


def _gelu(x):
    c = math.sqrt(2.0 / math.pi)
    return 0.5 * x * (1.0 + jnp.tanh(c * (x + 0.044715 * (x * x * x))))


def _peer_experts_kernel(x1_ref, xn_ref, i1_ref, i2_ref, g_ref, u_ref, v_ref, fg_ref, y_ref,
                         w_ref, acc_ref, *, tt, n1, unroll):
    c = pl.program_id(1)
    nc = pl.num_programs(1)
    nt = (((1,), (1,)), ((), ()))
    hi_mask = jnp.uint32(0xFFFF0000)
    half = I1_GROUP // 2

    @pl.when(c == 0)
    def _build_gates():
        row = lax.broadcasted_iota(jnp.int32, (N_KEYS, N_KEYS), 0)
        pr = row & (W_ROWS - 1)
        i1_of_row = ((pr // half) * I1_GROUP + (pr % half) + half * (row // W_ROWS)).astype(F32)
        sub = row.astype(F32)

        def one(t, carry):
            r1 = i1_ref[pl.ds(t, 1), :]
            r2 = i2_ref[pl.ds(t, 1), :]
            gg = g_ref[pl.ds(t, 1), :]
            lt = jnp.where(i1_of_row == r1, 1.0, 0.0).astype(BF16)
            rt = jnp.where(sub == r2, gg, 0.0).astype(BF16)
            w = lax.dot_general(lt, rt, nt, preferred_element_type=F32)
            bits = lax.bitcast_convert_type(w.astype(BF16).astype(F32), jnp.uint32)
            words = (bits[W_ROWS:] & hi_mask) | (bits[0:W_ROWS] >> 16)
            w_ref[pl.ds(pl.multiple_of(t * W_PITCH, 8), W_ROWS), :] = words
            return carry

        lax.fori_loop(0, tt, one, 0, unroll=unroll)
        acc_ref[...] = jnp.zeros(acc_ref.shape, F32)

    a = lax.dot_general(xn_ref[...], u_ref[...], nt, preferred_element_type=F32)
    lo, hi = [], []
    for g in range(n1 // I1_GROUP):
        for u in range(half):
            words = w_ref[pl.ds((c * (n1 // I1_GROUP) + g) * half + u, tt, stride=W_PITCH), :]
            lo.append(lax.bitcast_convert_type(words << 16, F32))
            hi.append(lax.bitcast_convert_type(words & hi_mask, F32))
    wc = jnp.concatenate(
        [part[g * half + u] for g in range(n1 // I1_GROUP) for part in (lo, hi) for u in range(half)], axis=1)
    z = (_gelu(a) * wc).astype(BF16)
    acc_ref[...] += jnp.dot(z, v_ref[...], preferred_element_type=F32)

    @pl.when(c == nc - 1)
    def _fin():
        y_ref[...] = _rms(x1_ref[...] + acc_ref[...], fg_ref[...])


def _peer_experts(x1, xn, i1, i2, gate, u_tab, v_tab, final_g, tt=512, n1=8, unroll=8):
    t, d = x1.shape
    ne = u_tab.shape[0]
    ec = n1 * N_KEYS
    tok = lambda i, c: (i, 0)
    kern = functools.partial(_peer_experts_kernel, tt=tt, n1=n1, unroll=unroll)
    return pl.pallas_call(
        kern,
        grid=(t // tt, ne // ec),
        in_specs=[pl.BlockSpec((tt, d), tok), pl.BlockSpec((tt, d), tok),
                  pl.BlockSpec((tt, LANES), tok), pl.BlockSpec((tt, LANES), tok), pl.BlockSpec((tt, LANES), tok),
                  pl.BlockSpec((ec, d), lambda i, c: (c, 0)),
                  pl.BlockSpec((ec, d), lambda i, c: (c, 0)),
                  pl.BlockSpec((1, d), lambda i, c: (0, 0))],
        out_specs=pl.BlockSpec((tt, d), tok),
        out_shape=jax.ShapeDtypeStruct((t, d), F32),
        scratch_shapes=[pltpu.VMEM((tt * W_PITCH, N_KEYS), jnp.uint32),
                        pltpu.VMEM((tt, d), F32)],
        compiler_params=_cparams(("parallel", "arbitrary")),
        name="peer_experts",
    )(x1, xn, i1, i2, gate, u_tab, v_tab, final_g.reshape(1, d))


def _trunk(x, norm1_g, w_in, lam_vecs, subln_g, w_out, norm2_g, wq, keys1, keys2, u_bf, v_bf, final_g):
    b, s, d = x.shape
    qa_t, ka, va_t, qb, kb, vb = _in_proj(x, norm1_g, w_in)
    oa = _diff_attn(qa_t, ka, va_t, lam_vecs, subln_g)
    obs, lses = [], []
    for window, dil in DIL_PATTERNS:
        o, lse = _dil_attn(qb, kb, vb, window, dil)
        obs.append(o)
        lses.append(lse)
    x1, xn2 = _out_proj(x, oa, obs, lses, w_out, norm2_g)
    x1, xn2 = x1.reshape(b * s, d), xn2.reshape(b * s, d)
    i1, i2, gate = _peer_select(xn2, wq, keys1, keys2)
    y = _peer_experts(x1, xn2, i1, i2, gate, u_bf, v_bf, final_g)
    return y.reshape(b, s, d)


def kernel(x_prompt, x_sample, norm1_g, w_in, lambda_q1, lambda_k1, lambda_q2, lambda_k2, subln_g, w_out,
           norm2_g, peer_wq, peer_keys1, peer_keys2, peer_u, peer_v, final_g):
    lam_vecs = jnp.concatenate([lambda_q1, lambda_k1, lambda_q2, lambda_k2], axis=0).astype(F32)
    u_bf = peer_u[0].astype(BF16)
    v_bf = peer_v[0].astype(BF16)
    args = (norm1_g[0], w_in[0], lam_vecs, subln_g[0], w_out[0], norm2_g[0], peer_wq[0],
            peer_keys1[0], peer_keys2[0], u_bf, v_bf, final_g)
    return (_trunk(x_prompt, *args), _trunk(x_sample, *args))
```

```python
import functools
import math

import jax
import jax.numpy as jnp
import numpy as np
from jax import lax
from jax.experimental import pallas as pl
from jax.experimental.pallas import tpu as pltpu

D_MODEL = 1024
HEAD_DIM = 64
DIFF_HEADS = 4
DIFF_V_DIM = 2 * HEAD_DIM
DIL_HEADS = 8
DIL_PATTERNS = ((128, 1), (512, 4), (2048, 16))
A_W = DIFF_HEADS * 2 * HEAD_DIM
B_W = DIL_HEADS * HEAD_DIM
PEER_HEADS = 8
N_KEYS = 128
PEER_HALF = 128
PEER_TOPK = 16
EPS = 1e-6
LAMBDA_INIT = 0.8 - 0.6 * math.exp(-0.3 * 0)

LOG2E = math.log2(math.e)
ATT_CHUNK = 256
VT_ROWS = DIFF_V_DIM + 16
LANES = 128
VMEM_LIMIT = 56 * 1024 * 1024

F32 = jnp.float32
BF16 = jnp.bfloat16
NEG_INF = float("-inf")


def _cparams(sem):
    return pltpu.CompilerParams(dimension_semantics=sem, vmem_limit_bytes=VMEM_LIMIT)


def _unrolled_loop(lo, hi, body, unroll):
    trips = (hi - lo) // unroll

    def group(g, carry):
        for u in range(unroll):
            body(lo + g * unroll + u, carry)
        return carry

    lax.fori_loop(0, trips, group, 0)
    lax.fori_loop(lo + trips * unroll, hi, body, 0)


def _rms(x, g):
    return x * lax.rsqrt(jnp.mean(x * x, axis=-1, keepdims=True) + EPS) * g


def _in_proj_kernel(x_ref, g_ref, wqa_t_ref, wva_t_ref, wrest_ref, qa_t_ref, ka_ref, va_t_ref, *rest_refs, tkc):
    dil_refs, slab_ref = rest_refs[:-1], rest_refs[-1]
    xn = _rms(x_ref[0], g_ref[...]).astype(BF16)
    tm = xn.shape[0]
    nt = (((1,), (1,)), ((), ()))
    qa_t_ref[0] = lax.dot_general(wqa_t_ref[...], xn, nt, preferred_element_type=F32).astype(BF16)
    va_t = lax.dot_general(wva_t_ref[...], xn, nt, preferred_element_type=F32).astype(BF16)
    ones = jnp.ones((VT_ROWS - DIFF_V_DIM, tkc), BF16)
    for hd in range(DIFF_HEADS):
        for c in range(tm // tkc):
            va_t_ref[0, hd, c, 0:DIFF_V_DIM, :] = va_t[hd * DIFF_V_DIM:(hd + 1) * DIFF_V_DIM, c * tkc:(c + 1) * tkc]
            va_t_ref[0, hd, c, DIFF_V_DIM:VT_ROWS, :] = ones
    rest = jnp.dot(xn, wrest_ref[...], preferred_element_type=F32)
    ka_ref[0] = rest[:, 0:A_W].astype(BF16)
    nsl = B_W // LANES
    for a in range(3):
        cols = rest[:, (1 + a) * B_W:(2 + a) * B_W]
        dil_refs[a][0] = cols.astype(BF16)
        for sl in range(nsl):
            slab_ref[a * nsl + sl] = cols[:, sl * LANES:(sl + 1) * LANES]
    for pi, (_, dil) in enumerate(DIL_PATTERNS[1:]):
        for a in range(3):
            out = dil_refs[3 * (pi + 1) + a]
            for r in range(dil):
                for sl in range(nsl):
                    rows = slab_ref[a * nsl + sl, pl.ds(r, tm // dil, stride=dil), :]
                    out[0, :, r * B_W + sl * LANES:r * B_W + (sl + 1) * LANES] = rows.astype(BF16)


def _in_proj(x, g, w_in, tm=512, tkc=ATT_CHUNK):
    b, s, d = x.shape
    assert DIL_PATTERNS[0][1] == 1
    scale = HEAD_DIM ** -0.5
    wqa_t = (w_in[:, 0:A_W] * (scale * LOG2E)).T.astype(BF16)
    wva_t = w_in[:, 2 * A_W:3 * A_W].T.astype(BF16)
    wrest = jnp.concatenate(
        [w_in[:, A_W:2 * A_W], w_in[:, 3 * A_W:4 * A_W] * scale, w_in[:, 4 * A_W:]], axis=1).astype(BF16)
    tok = lambda i, j: (i, j, 0)
    tr = lambda i, j: (i, 0, j)
    const = lambda i, j: (0, 0)
    dil_specs, dil_shapes = [], []
    for _, dil in DIL_PATTERNS:
        dil_specs += [pl.BlockSpec((1, tm // dil, dil * B_W), tok)] * 3
        dil_shapes += [jax.ShapeDtypeStruct((b, s // dil, dil * B_W), BF16)] * 3
    outs = pl.pallas_call(
        functools.partial(_in_proj_kernel, tkc=tkc),
        grid=(b, s // tm),
        in_specs=[pl.BlockSpec((1, tm, d), tok),
                  pl.BlockSpec((1, d), const),
                  pl.BlockSpec((A_W, d), const),
                  pl.BlockSpec((A_W, d), const),
                  pl.BlockSpec((d, 4 * A_W), const)],
        out_specs=[pl.BlockSpec((1, A_W, tm), tr),
                   pl.BlockSpec((1, tm, A_W), tok),
                   pl.BlockSpec((1, DIFF_HEADS, tm // tkc, VT_ROWS, tkc), lambda i, j: (i, 0, j, 0, 0))] + dil_specs,
        out_shape=[jax.ShapeDtypeStruct((b, A_W, s), BF16),
                   jax.ShapeDtypeStruct((b, s, A_W), BF16),
                   jax.ShapeDtypeStruct((b, DIFF_HEADS, s // tkc, VT_ROWS, tkc), BF16)] + dil_shapes,
        scratch_shapes=[pltpu.VMEM((3 * B_W // LANES, tm, LANES), F32)],
        compiler_params=_cparams(("parallel", "parallel")),
        name="in_proj",
    )(x, g.reshape(1, d), wqa_t, wva_t, wrest)
    qkv = [tuple(outs[3 + 3 * p:6 + 3 * p]) for p in range(len(DIL_PATTERNS))]
    return outs[0], outs[1], outs[2], qkv


def _bf16_split3(x):
    x = np.asarray(x, np.float32)
    pieces = []
    for _ in range(3):
        p = x.astype(jnp.bfloat16).astype(np.float32)
        pieces.append(p.astype(np.float64))
        x = (x - p).astype(np.float32)
    return pieces


def _alibi_tables(tq, tkc):
    slopes = 2.0 ** (-8.0 * np.arange(1, DIFF_HEADS + 1) / DIFF_HEADS)
    a = (slopes * LOG2E).astype(np.float32)
    kx = np.zeros((tkc, LANES), np.float32)
    s_rel = np.arange(tkc, dtype=np.float32)
    kx[:, 0:3] = s_rel[:, None]
    kx[:, 3:6] = 1.0
    qx = np.zeros((DIFF_HEADS, 2, LANES, tq), np.float64)
    t_rel = np.arange(tq, dtype=np.float32)
    for hd in range(DIFF_HEADS):
        a3 = _bf16_split3(a[hd])
        f3 = _bf16_split3(-(a[hd] * t_rel).astype(np.float32))
        for r in range(3):
            qx[hd, 0, r, :] = a3[r]
            qx[hd, 0, 3 + r, :] = f3[r]
    qx[:, 1] = -qx[:, 0]
    rel = s_rel[:, None] - t_rel[None, :]
    ndiag = tq // tkc
    diag = np.stack([np.stack([-(a[hd] * np.abs(rel + d * tkc)) for d in range(ndiag)])
                     for hd in range(DIFF_HEADS)]).astype(np.float32)
    return (jnp.asarray(a), jnp.asarray(kx, F32).astype(BF16), jnp.asarray(qx, F32).astype(BF16),
            jnp.asarray(diag))


def _diff_attn_kernel(a_ref, qt_ref, k_ref, vt_ref, kx_ref, qx_ref, diag_ref, lam_ref, sg_ref, o_ref,
                      qs_ref, m_ref, acc_ref, s_ref, *, tq, tkc, unroll):
    h = pl.program_id(1)
    i = pl.program_id(2)
    nk = k_ref.shape[1] // tkc
    ndiag = tq // tkc
    a_h = a_ref[h]

    qt = qt_ref[0]
    zero = jnp.zeros((HEAD_DIM, tq), BF16)
    for side in range(2):
        qs_ref[side, 0:HEAD_DIM, 0:tq] = qt[0:HEAD_DIM]
        qs_ref[side, 0:HEAD_DIM, tq:2 * tq] = zero
        qs_ref[side, HEAD_DIM:DIFF_V_DIM, 0:tq] = zero
        qs_ref[side, HEAD_DIM:DIFF_V_DIM, tq:2 * tq] = qt[HEAD_DIM:]
        qs_ref[side, DIFF_V_DIM:, 0:tq] = qx_ref[0, side]
        qs_ref[side, DIFF_V_DIM:, tq:2 * tq] = qx_ref[0, side]
    m_ref[...] = jnp.full(m_ref.shape, NEG_INF, F32)
    acc_ref[...] = jnp.zeros(acc_ref.shape, F32)

    def update(j, s, c):
        m_old = m_ref[...]
        m_new = jnp.maximum(m_old, jnp.max(s, axis=0, keepdims=True) + c)
        alpha = jnp.exp2(m_old - m_new)
        p = jnp.exp2(s - (m_new - c)).astype(BF16)
        acc_ref[...] = alpha * acc_ref[...] + jnp.dot(vt_ref[0, 0, j], p, preferred_element_type=F32)
        m_ref[...] = m_new

    n_left = i * ndiag
    for d in range(ndiag):
        j = n_left + d
        k0 = pl.multiple_of(j * tkc, tkc)
        s = jnp.dot(k_ref[0, pl.ds(k0, tkc), :], qs_ref[0, 0:DIFF_V_DIM, :], preferred_element_type=F32)
        bias = diag_ref[0, d]
        update(j, s + jnp.concatenate([bias, bias], axis=1), jnp.float32(0.0))

    cnt = nk - ndiag
    last = cnt - 1

    def chunk_of(n):
        n = jnp.minimum(n, last)
        side = (n >= n_left).astype(jnp.int32)
        j = n + side * ndiag
        dist = (1 - 2 * side) * (i * tq - j * tkc)
        return j, side, -a_h * dist.astype(F32)

    def scores(n, slot):
        j, side, _ = chunk_of(n)
        k0 = pl.multiple_of(j * tkc, tkc)
        lhs = jnp.concatenate([k_ref[0, pl.ds(k0, tkc), :], kx_ref[...]], axis=1)
        s_ref[slot] = jnp.dot(lhs, qs_ref[side], preferred_element_type=F32)

    def consume(n, slot):
        j, _, c = chunk_of(n)
        update(j, s_ref[slot], c)

    scores(0, 0)

    def group(g, carry):
        for u in range(unroll):
            scores(g * unroll + u + 1, (u + 1) % 2)
            consume(g * unroll + u, u % 2)
        return carry

    trips = cnt // unroll
    lax.fori_loop(0, trips, group, 0)
    for n in range(trips * unroll, cnt):
        if n + 1 < cnt:
            scores(n + 1, (n + 1) % 2)
        consume(n, n % 2)

    lam_v = lam_ref[...]
    lam = (jnp.exp(jnp.sum(lam_v[0:1] * lam_v[1:2], axis=1, keepdims=True))
           - jnp.exp(jnp.sum(lam_v[2:3] * lam_v[3:4], axis=1, keepdims=True)) + LAMBDA_INIT)
    acc = acc_ref[...]
    o = acc[0:DIFF_V_DIM] / acc[DIFF_V_DIM:DIFF_V_DIM + 1]
    o = o[:, 0:tq] - lam * o[:, tq:2 * tq]
    ms = jnp.mean(o * o, axis=0, keepdims=True)
    o = o * lax.rsqrt(ms + EPS) * sg_ref[...] * (1.0 - LAMBDA_INIT)
    o_ref[0] = o.T.astype(BF16)


def _diff_attn(qa_t, ka, va_t, lam_vecs, subln_g, tq=512, tkc=ATT_CHUNK, unroll=4):
    b, _, s = qa_t.shape
    nk = s // tkc
    a, kx, qx, diag = _alibi_tables(tq, tkc)
    kern = functools.partial(_diff_attn_kernel, tq=tq, tkc=tkc, unroll=unroll)
    c2 = lambda b_, h, i: (0, 0)
    return pl.pallas_call(
        kern,
        grid=(b, DIFF_HEADS, s // tq),
        in_specs=[pl.BlockSpec(memory_space=pltpu.SMEM),
                  pl.BlockSpec((1, DIFF_V_DIM, tq), lambda b_, h, i: (b_, h, i)),
                  pl.BlockSpec((1, s, DIFF_V_DIM), lambda b_, h, i: (b_, 0, h)),
                  pl.BlockSpec((1, 1, nk, VT_ROWS, tkc), lambda b_, h, i: (b_, h, 0, 0, 0)),
                  pl.BlockSpec((tkc, LANES), c2),
                  pl.BlockSpec((1, 2, LANES, tq), lambda b_, h, i: (h, 0, 0, 0)),
                  pl.BlockSpec((1, tq // tkc, tkc, tq), lambda b_, h, i: (h, 0, 0, 0)),
                  pl.BlockSpec((4, HEAD_DIM), c2),
                  pl.BlockSpec((DIFF_V_DIM, 1), c2)],
        out_specs=pl.BlockSpec((1, tq, DIFF_V_DIM), lambda b_, h, i: (b_, i, h)),
        out_shape=jax.ShapeDtypeStruct((b, s, A_W), BF16),
        scratch_shapes=[pltpu.VMEM((2, 2 * LANES, 2 * tq), BF16),
                        pltpu.VMEM((1, 2 * tq), F32),
                        pltpu.VMEM((VT_ROWS, 2 * tq), F32),
                        pltpu.VMEM((2, tkc, 2 * tq), F32)],
        compiler_params=_cparams(("parallel", "parallel", "arbitrary")),
        name="diff_attn",
    )(a, qa_t, ka, va_t, kx, qx, diag, lam_vecs, subln_g.reshape(DIFF_V_DIM, 1))


def _dil_attn_kernel(bias_ref, q_ref, kp_ref, kc_ref, kn_ref, vp_ref, vc_ref, vn_ref,
                     o_ref, lse_ref, *, n_sub, blk, qt):
    i = pl.program_id(2)
    nsb = qt // blk
    col = lax.broadcasted_iota(jnp.int32, (1, 3 * blk), 1)
    lane = lax.broadcasted_iota(jnp.int32, (blk, LANES), 1)
    first = lane < HEAD_DIM
    nt = (((1,), (1,)), ((), ()))

    def window(sb, prev_ref, cur_ref, next_ref, cols):
        parts = [prev_ref[0, :, cols] if sb == 0 else cur_ref[0, (sb - 1) * blk:sb * blk, cols],
                 cur_ref[0, sb * blk:(sb + 1) * blk, cols],
                 next_ref[0, :, cols] if sb == nsb - 1 else cur_ref[0, (sb + 1) * blk:(sb + 2) * blk, cols]]
        return jnp.concatenate(parts, axis=0)

    def scores(sb, hp):
        cols = slice(hp * LANES, (hp + 1) * LANES)
        q = q_ref[0, sb * blk:(sb + 1) * blk, cols]
        zq = jnp.zeros_like(q)
        q2 = jnp.concatenate([jnp.where(first, q, zq), jnp.where(first, zq, q)], axis=0)
        return lax.dot_general(q2, window(sb, kp_ref, kc_ref, kn_ref, cols), nt, preferred_element_type=F32)

    def finish(sb, hp, s):
        cols = slice(hp * LANES, (hp + 1) * LANES)
        s = s + jnp.concatenate([bias_ref[2 * hp], bias_ref[2 * hp + 1]], axis=0)
        if sb == 0 or sb == nsb - 1:
            kidx = i * qt + (sb - 1) * blk + col
            s = jnp.where((kidx >= 0) & (kidx < n_sub), s, NEG_INF)
        m = jnp.max(s, axis=1, keepdims=True)
        p = jnp.exp(s - m)
        l = jnp.sum(p, axis=1, keepdims=True)
        o = jnp.dot(p.astype(BF16), window(sb, vp_ref, vc_ref, vn_ref, cols), preferred_element_type=F32) / l
        lse = jnp.broadcast_to(m + jnp.log(l), (2 * blk, LANES))
        o_ref[0, sb * blk:(sb + 1) * blk, cols] = jnp.where(first, o[0:blk], o[blk:]).astype(BF16)
        lse_ref[0, sb * blk:(sb + 1) * blk, cols] = jnp.where(first, lse[0:blk], lse[blk:])

    units = [(sb, hp) for sb in range(nsb) for hp in range(DIL_HEADS // 2)]
    pending = scores(*units[0])
    for n, unit in enumerate(units):
        nxt = scores(*units[n + 1]) if n + 1 < len(units) else None
        finish(*unit, pending)
        pending = nxt


def _dil_attn(q2, k2, v2, window, dil, blk=128, qt=512):
    shape2 = q2.shape
    b, n_sub, w = shape2[0], shape2[1], shape2[2] // dil
    qt = min(qt, n_sub)
    nq = n_sub // qt
    nblk = n_sub // blk
    half = window // (2 * dil)
    slopes = 2.0 ** (-8.0 * np.arange(1, DIL_HEADS + 1) / DIL_HEADS)
    joff = np.arange(3 * blk)[None, :] - blk - np.arange(blk)[:, None]
    band = np.abs(joff) <= half
    bias = np.where(band[None], -slopes[:, None, None] * (np.abs(joff) * dil)[None], -np.inf).astype(np.float32)
    cur = lambda b_, r, i: (b_, i, r)
    prv = lambda b_, r, i: (b_, jnp.maximum(i * (qt // blk) - 1, 0), r)
    nxt = lambda b_, r, i: (b_, jnp.minimum((i + 1) * (qt // blk), nblk - 1), r)
    big = lambda f: pl.BlockSpec((1, qt, w), f)
    halo = lambda f: pl.BlockSpec((1, blk, w), f)
    kern = functools.partial(_dil_attn_kernel, n_sub=n_sub, blk=blk, qt=qt)
    o, lse = pl.pallas_call(
        kern,
        grid=(b, dil, nq),
        in_specs=[pl.BlockSpec((DIL_HEADS, blk, 3 * blk), lambda b_, r, i: (0, 0, 0)),
                  big(cur), halo(prv), big(cur), halo(nxt), halo(prv), big(cur), halo(nxt)],
        out_specs=[big(cur), big(cur)],
        out_shape=[jax.ShapeDtypeStruct(shape2, BF16), jax.ShapeDtypeStruct(shape2, F32)],
        compiler_params=_cparams(("parallel", "parallel", "parallel")),
        name=f"dil_attn_d{dil}",
    )(jnp.asarray(bias), q2, k2, k2, k2, v2, v2, v2)
    return o, lse


def _out_proj_kernel(x_ref, oa_ref, o1_ref, o2_ref, o3_ref, l1_ref, l2_ref, l3_ref,
                     wa_ref, wb_ref, g_ref, x1_ref, xn_ref, slab_ref):
    tm = x_ref.shape[1]
    nsl = B_W // LANES

    def natural(ref, dil, base):
        if dil == 1:
            return ref[0].astype(F32)
        for r in range(dil):
            for sl in range(nsl):
                slab_ref[base + sl, pl.ds(r, tm // dil, stride=dil), :] = (
                    ref[0, :, r * B_W + sl * LANES:r * B_W + (sl + 1) * LANES].astype(F32))
        return jnp.concatenate([slab_ref[base + sl] for sl in range(nsl)], axis=1)

    dils = [dil for _, dil in DIL_PATTERNS]
    l1, l2, l3 = [natural(ref, dil, (2 * n) * nsl) for n, (ref, dil) in enumerate(zip((l1_ref, l2_ref, l3_ref), dils))]
    o1, o2, o3 = [natural(ref, dil, (2 * n + 1) * nsl) for n, (ref, dil) in enumerate(zip((o1_ref, o2_ref, o3_ref), dils))]
    m = jnp.maximum(jnp.maximum(l1, l2), l3)
    e1, e2, e3 = jnp.exp(l1 - m), jnp.exp(l2 - m), jnp.exp(l3 - m)
    ob = (e1 * o1 + e2 * o2 + e3 * o3) / (e1 + e2 + e3)
    y = (x_ref[0]
         + jnp.dot(oa_ref[0], wa_ref[...], preferred_element_type=F32)
         + jnp.dot(ob.astype(BF16), wb_ref[...], preferred_element_type=F32))
    x1_ref[0] = y
    xn_ref[0] = _rms(y, g_ref[...]).astype(BF16)


def _out_proj(x, oa, obs, lses, w_out, g, tm=512):
    b, s, d = x.shape
    wa = w_out[0:A_W].astype(BF16)
    wb = w_out[A_W:].astype(BF16)
    tok = lambda i, j: (i, j, 0)
    const = lambda i, j: (0, 0)
    dil_specs = [pl.BlockSpec((1, tm // dil, dil * B_W), tok) for _, dil in DIL_PATTERNS]
    return pl.pallas_call(
        _out_proj_kernel,
        grid=(b, s // tm),
        in_specs=[pl.BlockSpec((1, tm, d), tok), pl.BlockSpec((1, tm, A_W), tok)] + dil_specs + dil_specs +
                 [pl.BlockSpec((A_W, d), const), pl.BlockSpec((B_W, d), const), pl.BlockSpec((1, d), const)],
        out_specs=[pl.BlockSpec((1, tm, d), tok), pl.BlockSpec((1, tm, d), tok)],
        out_shape=[jax.ShapeDtypeStruct((b, s, d), F32), jax.ShapeDtypeStruct((b, s, d), BF16)],
        scratch_shapes=[pltpu.VMEM((2 * len(DIL_PATTERNS) * B_W // LANES, tm, LANES), F32)],
        compiler_params=_cparams(("parallel", "parallel")),
        name="out_proj",
    )(x, oa, *obs, *lses, wa, wb, g.reshape(1, d))


def _cand_tables():
    groups = [(0, 16), (1, 8)] + [(a, 8) for a in range(2, 8)]
    a_idx, b_idx = [], []
    for a, n in groups:
        a_idx += [a] * n
        b_idx += list(range(n))
    a_idx += list(range(8, 16))
    b_idx += [0] * 8
    a_idx, b_idx = np.array(a_idx), np.array(b_idx)
    ok = (a_idx + 1) * (b_idx + 1) <= PEER_TOPK
    pos = np.where(ok, a_idx * PEER_TOPK + b_idx, -1)
    return groups, pos.astype(np.int32)


def _top_rows(s, k):
    n = s.shape[0]
    iota = lax.broadcasted_iota(jnp.int32, s.shape, 0)
    vals, idxs = [], []
    for _ in range(k):
        m = jnp.max(s, axis=0, keepdims=True)
        ix = jnp.min(jnp.where(s == m, iota, n), axis=0, keepdims=True)
        vals.append(m)
        idxs.append(ix)
        s = jnp.where(iota == ix, NEG_INF, s)
    return jnp.concatenate(vals, axis=0), jnp.concatenate(idxs, axis=0)


def _peer_select_kernel(xn_ref, wq_t_ref, keys_ref, pos_ref, i1_ref, i2_ref, g_ref,
                        q_ref, v_ref, ix_ref, e_ref, gt_ref, *, groups):
    nt = (((1,), (1,)), ((), ()))
    q_ref[...] = lax.dot_general(wq_t_ref[...], xn_ref[...], nt,
                                 preferred_element_type=F32).astype(BF16)
    tt = xn_ref.shape[0]

    def half_topk(hc, carry):
        qh = q_ref[pl.ds(pl.multiple_of(hc * PEER_HALF, PEER_HALF), PEER_HALF), :]
        s = jnp.dot(keys_ref[hc % 2], qh, preferred_element_type=F32)
        vals, idxs = _top_rows(s, PEER_TOPK)
        v_ref[hc] = vals
        ix_ref[hc] = idxs
        return carry

    lax.fori_loop(0, 2 * PEER_HEADS, half_topk, 0, unroll=2)

    pos_tab = pos_ref[...]
    big = jnp.int32(1 << 30)

    def head_select(h, carry):
        v1, v2 = v_ref[2 * h], v_ref[2 * h + 1]
        i1, i2 = ix_ref[2 * h], ix_ref[2 * h + 1]
        v1g, v2g, i1g, i2g = [], [], [], []
        for a, n in groups:
            v1g.append(jnp.broadcast_to(v1[a:a + 1], (n, tt)))
            i1g.append(jnp.broadcast_to(i1[a:a + 1], (n, tt)))
            v2g.append(v2[0:n])
            i2g.append(i2[0:n])
        v1g.append(v1[8:16]); i1g.append(i1[8:16])
        v2g.append(jnp.broadcast_to(v2[0:1], (8, tt))); i2g.append(jnp.broadcast_to(i2[0:1], (8, tt)))
        cand = jnp.concatenate(v1g, axis=0) + jnp.concatenate(v2g, axis=0)
        eidx = jnp.concatenate(i1g, axis=0) * N_KEYS + jnp.concatenate(i2g, axis=0)
        cand = jnp.where(pos_tab >= 0, cand, NEG_INF)
        key = pos_tab * (N_KEYS * N_KEYS) + eidx
        tops, sel = [], []
        for _ in range(PEER_TOPK):
            m = jnp.max(cand, axis=0, keepdims=True)
            kk = jnp.min(jnp.where(cand == m, key, big), axis=0, keepdims=True)
            tops.append(m)
            sel.append(kk)
            cand = jnp.where(key == kk, NEG_INF, cand)
        top = jnp.concatenate(tops, axis=0)
        ek = jnp.concatenate(sel, axis=0) & (N_KEYS * N_KEYS - 1)
        ex = jnp.exp(top - top[0:1])
        gate = ex / jnp.sum(ex, axis=0, keepdims=True)
        e_ref[h] = ek.astype(F32)
        gt_ref[h] = gate
        return carry

    lax.fori_loop(0, PEER_HEADS, head_select, 0, unroll=2)

    ef = e_ref[...].reshape(PEER_HEADS * PEER_TOPK, tt)
    i1f = jnp.floor(ef * (1.0 / N_KEYS))
    i1_ref[...] = i1f.T
    i2_ref[...] = (ef - i1f * N_KEYS).T
    g_ref[...] = gt_ref[...].reshape(PEER_HEADS * PEER_TOPK, tt).T


def _peer_select(xn, wq, keys1, keys2, tt=256):
    t, d = xn.shape
    groups, pos = _cand_tables()
    pos_tab = jnp.broadcast_to(jnp.asarray(pos)[:, None], (pos.shape[0], tt))
    nslot = PEER_HEADS * PEER_TOPK
    qw = 2 * PEER_HEADS * PEER_HALF
    const = lambda i: (0, 0)
    out = pl.BlockSpec((tt, nslot), lambda i: (i, 0))
    kern = functools.partial(_peer_select_kernel, groups=groups)
    return pl.pallas_call(
        kern,
        grid=(t // tt,),
        in_specs=[pl.BlockSpec((tt, d), lambda i: (i, 0)),
                  pl.BlockSpec((qw, d), const),
                  pl.BlockSpec((2, N_KEYS, PEER_HALF), lambda i: (0, 0, 0)),
                  pl.BlockSpec((pos.shape[0], tt), const)],
        out_specs=[out, out, out],
        out_shape=[jax.ShapeDtypeStruct((t, nslot), F32)] * 3,
        scratch_shapes=[pltpu.VMEM((qw, tt), BF16),
                        pltpu.VMEM((2 * PEER_HEADS, PEER_TOPK, tt), F32),
                        pltpu.VMEM((2 * PEER_HEADS, PEER_TOPK, tt), jnp.int32),
                        pltpu.VMEM((PEER_HEADS, PEER_TOPK, tt), F32),
                        pltpu.VMEM((PEER_HEADS, PEER_TOPK, tt), F32)],
        compiler_params=_cparams(("parallel",)),
        name="peer_select",
    )(xn, wq.T.astype(BF16), jnp.stack([keys1, keys2]).astype(BF16), pos_tab)


W_ROWS = N_KEYS // 2
W_PITCH = W_ROWS + 8
I1_GROUP = 8


def _gelu(x):
    c = math.sqrt(2.0 / math.pi)
    return 0.5 * x * (1.0 + jnp.tanh(c * (x + 0.044715 * (x * x * x))))


def _peer_experts_kernel(x1_ref, xn_ref, i1_ref, i2_ref, g_ref, u_ref, v_ref, fg_ref, y_ref,
                         w_ref, acc_ref, *, tt, n1, unroll):
    c = pl.program_id(1)
    nc = pl.num_programs(1)
    nt = (((1,), (1,)), ((), ()))
    hi_mask = jnp.uint32(0xFFFF0000)
    half = I1_GROUP // 2

    @pl.when(c == 0)
    def _build_gates():
        row = lax.broadcasted_iota(jnp.int32, (N_KEYS, N_KEYS), 0)
        pr = row & (W_ROWS - 1)
        i1_of_row = ((pr // half) * I1_GROUP + (pr % half) + half * (row // W_ROWS)).astype(F32)
        sub = row.astype(F32)

        def one(t, carry):
            r1 = i1_ref[pl.ds(t, 1), :]
            r2 = i2_ref[pl.ds(t, 1), :]
            gg = g_ref[pl.ds(t, 1), :]
            lt = jnp.where(i1_of_row == r1, 1.0, 0.0).astype(BF16)
            rt = jnp.where(sub == r2, gg, 0.0).astype(BF16)
            w = lax.dot_general(lt, rt, nt, preferred_element_type=F32)
            bits = lax.bitcast_convert_type(w.astype(BF16).astype(F32), jnp.uint32)
            words = (bits[W_ROWS:] & hi_mask) | (bits[0:W_ROWS] >> 16)
            w_ref[pl.ds(pl.multiple_of(t * W_PITCH, 8), W_ROWS), :] = words
            return carry

        lax.fori_loop(0, tt, one, 0, unroll=unroll)
        acc_ref[...] = jnp.zeros(acc_ref.shape, F32)

    a = lax.dot_general(xn_ref[...], u_ref[...], nt, preferred_element_type=F32)
    lo, hi = [], []
    for g in range(n1 // I1_GROUP):
        for u in range(half):
            words = w_ref[pl.ds((c * (n1 // I1_GROUP) + g) * half + u, tt, stride=W_PITCH), :]
            lo.append(lax.bitcast_convert_type(words << 16, F32))
            hi.append(lax.bitcast_convert_type(words & hi_mask, F32))
    wc = jnp.concatenate(
        [part[g * half + u] for g in range(n1 // I1_GROUP) for part in (lo, hi) for u in range(half)], axis=1)
    z = (_gelu(a) * wc).astype(BF16)
    acc_ref[...] += jnp.dot(z, v_ref[...], preferred_element_type=F32)

    @pl.when(c == nc - 1)
    def _fin():
        y_ref[...] = _rms(x1_ref[...] + acc_ref[...], fg_ref[...])


def _peer_experts(x1, xn, i1, i2, gate, u_tab, v_tab, final_g, tt=512, n1=16, unroll=8):
    t, d = x1.shape
    ne = u_tab.shape[0]
    ec = n1 * N_KEYS
    tok = lambda i, c: (i, 0)
    kern = functools.partial(_peer_experts_kernel, tt=tt, n1=n1, unroll=unroll)
    return pl.pallas_call(
        kern,
        grid=(t // tt, ne // ec),
        in_specs=[pl.BlockSpec((tt, d), tok), pl.BlockSpec((tt, d), tok),
                  pl.BlockSpec((tt, LANES), tok), pl.BlockSpec((tt, LANES), tok), pl.BlockSpec((tt, LANES), tok),
                  pl.BlockSpec((ec, d), lambda i, c: (c, 0)),
                  pl.BlockSpec((ec, d), lambda i, c: (c, 0)),
                  pl.BlockSpec((1, d), lambda i, c: (0, 0))],
        out_specs=pl.BlockSpec((tt, d), tok),
        out_shape=jax.ShapeDtypeStruct((t, d), F32),
        scratch_shapes=[pltpu.VMEM((tt * W_PITCH, N_KEYS), jnp.uint32),
                        pltpu.VMEM((tt, d), F32)],
        compiler_params=_cparams(("parallel", "arbitrary")),
        name="peer_experts",
    )(x1, xn, i1, i2, gate, u_tab, v_tab, final_g.reshape(1, d))


def _trunk(x, norm1_g, w_in, lam_vecs, subln_g, w_out, norm2_g, wq, keys1, keys2, u_bf, v_bf, final_g):
    b, s, d = x.shape
    qa_t, ka, va_t, qkv = _in_proj(x, norm1_g, w_in)
    oa = _diff_attn(qa_t, ka, va_t, lam_vecs, subln_g)
    obs, lses = [], []
    for (window, dil), (q2, k2, v2) in zip(DIL_PATTERNS, qkv):
        o, lse = _dil_attn(q2, k2, v2, window, dil)
        obs.append(o)
        lses.append(lse)
    x1, xn2 = _out_proj(x, oa, obs, lses, w_out, norm2_g)
    x1, xn2 = x1.reshape(b * s, d), xn2.reshape(b * s, d)
    i1, i2, gate = _peer_select(xn2, wq, keys1, keys2)
    y = _peer_experts(x1, xn2, i1, i2, gate, u_bf, v_bf, final_g)
    return y.reshape(b, s, d)


def kernel(x_prompt, x_sample, norm1_g, w_in, lambda_q1, lambda_k1, lambda_q2, lambda_k2, subln_g, w_out,
           norm2_g, peer_wq, peer_keys1, peer_keys2, peer_u, peer_v, final_g):
    lam_vecs = jnp.concatenate([lambda_q1, lambda_k1, lambda_q2, lambda_k2], axis=0).astype(F32)
    u_bf = peer_u[0].astype(BF16)
    v_bf = peer_v[0].astype(BF16)
    args = (norm1_g[0], w_in[0], lam_vecs, subln_g[0], w_out[0], norm2_g[0], peer_wq[0],
            peer_keys1[0], peer_keys2[0], u_bf, v_bf, final_g)
    return (_trunk(x_prompt, *args), _trunk(x_sample, *args))
```

```python
import functools
import math

import jax
import jax.numpy as jnp
import numpy as np
from jax import lax
from jax.experimental import pallas as pl
from jax.experimental.pallas import tpu as pltpu

D_MODEL = 1024
HEAD_DIM = 64
DIFF_HEADS = 4
DIFF_V_DIM = 2 * HEAD_DIM
DIL_HEADS = 8
DIL_PATTERNS = ((128, 1), (512, 4), (2048, 16))
A_W = DIFF_HEADS * 2 * HEAD_DIM
B_W = DIL_HEADS * HEAD_DIM
PEER_HEADS = 8
N_KEYS = 128
PEER_HALF = 128
PEER_TOPK = 16
EPS = 1e-6
LAMBDA_INIT = 0.8 - 0.6 * math.exp(-0.3 * 0)

LOG2E = math.log2(math.e)
ATT_CHUNK = 256
VT_ROWS = DIFF_V_DIM + 16
LANES = 128
VMEM_LIMIT = 56 * 1024 * 1024

F32 = jnp.float32
BF16 = jnp.bfloat16
NEG_INF = float("-inf")


def _cparams(sem):
    return pltpu.CompilerParams(dimension_semantics=sem, vmem_limit_bytes=VMEM_LIMIT)


def _unrolled_loop(lo, hi, body, unroll):
    trips = (hi - lo) // unroll

    def group(g, carry):
        for u in range(unroll):
            body(lo + g * unroll + u, carry)
        return carry

    lax.fori_loop(0, trips, group, 0)
    lax.fori_loop(lo + trips * unroll, hi, body, 0)


def _rms(x, g):
    return x * lax.rsqrt(jnp.mean(x * x, axis=-1, keepdims=True) + EPS) * g


def _in_proj_kernel(x_ref, g_ref, wqa_t_ref, wva_t_ref, wrest_ref, qa_t_ref, ka_ref, va_t_ref, *rest_refs, tkc):
    dil_refs, slab_ref = rest_refs[:-1], rest_refs[-1]
    xn = _rms(x_ref[0], g_ref[...]).astype(BF16)
    tm = xn.shape[0]
    nt = (((1,), (1,)), ((), ()))
    qa_t_ref[0] = lax.dot_general(wqa_t_ref[...], xn, nt, preferred_element_type=F32).astype(BF16)
    va_t = lax.dot_general(wva_t_ref[...], xn, nt, preferred_element_type=F32).astype(BF16)
    ones = jnp.ones((VT_ROWS - DIFF_V_DIM, tkc), BF16)
    for hd in range(DIFF_HEADS):
        for c in range(tm // tkc):
            va_t_ref[0, hd, c, 0:DIFF_V_DIM, :] = va_t[hd * DIFF_V_DIM:(hd + 1) * DIFF_V_DIM, c * tkc:(c + 1) * tkc]
            va_t_ref[0, hd, c, DIFF_V_DIM:VT_ROWS, :] = ones
    rest = jnp.dot(xn, wrest_ref[...], preferred_element_type=F32)
    ka_ref[0] = rest[:, 0:A_W].astype(BF16)
    nsl = B_W // LANES
    for a in range(3):
        cols = rest[:, (1 + a) * B_W:(2 + a) * B_W]
        dil_refs[a][0] = cols.astype(BF16)
        for sl in range(nsl):
            slab_ref[a * nsl + sl] = cols[:, sl * LANES:(sl + 1) * LANES]
    for pi, (_, dil) in enumerate(DIL_PATTERNS[1:]):
        for a in range(3):
            out = dil_refs[3 * (pi + 1) + a]
            for r in range(dil):
                for sl in range(nsl):
                    rows = slab_ref[a * nsl + sl, pl.ds(r, tm // dil, stride=dil), :]
                    out[0, :, r * B_W + sl * LANES:r * B_W + (sl + 1) * LANES] = rows.astype(BF16)


def _in_proj(x, g, w_in, tm=512, tkc=ATT_CHUNK):
    b, s, d = x.shape
    assert DIL_PATTERNS[0][1] == 1
    scale = HEAD_DIM ** -0.5
    wqa_t = (w_in[:, 0:A_W] * (scale * LOG2E)).T.astype(BF16)
    wva_t = w_in[:, 2 * A_W:3 * A_W].T.astype(BF16)
    wrest = jnp.concatenate(
        [w_in[:, A_W:2 * A_W], w_in[:, 3 * A_W:4 * A_W] * scale, w_in[:, 4 * A_W:]], axis=1).astype(BF16)
    tok = lambda i, j: (i, j, 0)
    tr = lambda i, j: (i, 0, j)
    const = lambda i, j: (0, 0)
    dil_specs, dil_shapes = [], []
    for _, dil in DIL_PATTERNS:
        dil_specs += [pl.BlockSpec((1, tm // dil, dil * B_W), tok)] * 3
        dil_shapes += [jax.ShapeDtypeStruct((b, s // dil, dil * B_W), BF16)] * 3
    outs = pl.pallas_call(
        functools.partial(_in_proj_kernel, tkc=tkc),
        grid=(b, s // tm),
        in_specs=[pl.BlockSpec((1, tm, d), tok),
                  pl.BlockSpec((1, d), const),
                  pl.BlockSpec((A_W, d), const),
                  pl.BlockSpec((A_W, d), const),
                  pl.BlockSpec((d, 4 * A_W), const)],
        out_specs=[pl.BlockSpec((1, A_W, tm), tr),
                   pl.BlockSpec((1, tm, A_W), tok),
                   pl.BlockSpec((1, DIFF_HEADS, tm // tkc, VT_ROWS, tkc), lambda i, j: (i, 0, j, 0, 0))] + dil_specs,
        out_shape=[jax.ShapeDtypeStruct((b, A_W, s), BF16),
                   jax.ShapeDtypeStruct((b, s, A_W), BF16),
                   jax.ShapeDtypeStruct((b, DIFF_HEADS, s // tkc, VT_ROWS, tkc), BF16)] + dil_shapes,
        scratch_shapes=[pltpu.VMEM((3 * B_W // LANES, tm, LANES), F32)],
        compiler_params=_cparams(("parallel", "parallel")),
        name="in_proj",
    )(x, g.reshape(1, d), wqa_t, wva_t, wrest)
    qkv = [tuple(outs[3 + 3 * p:6 + 3 * p]) for p in range(len(DIL_PATTERNS))]
    return outs[0], outs[1], outs[2], qkv


def _bf16_split3(x):
    x = np.asarray(x, np.float32)
    pieces = []
    for _ in range(3):
        p = x.astype(jnp.bfloat16).astype(np.float32)
        pieces.append(p.astype(np.float64))
        x = (x - p).astype(np.float32)
    return pieces


def _alibi_tables(tq, tkc):
    slopes = 2.0 ** (-8.0 * np.arange(1, DIFF_HEADS + 1) / DIFF_HEADS)
    a = (slopes * LOG2E).astype(np.float32)
    kx = np.zeros((tkc, LANES), np.float32)
    s_rel = np.arange(tkc, dtype=np.float32)
    kx[:, 0:3] = s_rel[:, None]
    kx[:, 3:6] = 1.0
    qx = np.zeros((DIFF_HEADS, 2, LANES, tq), np.float64)
    t_rel = np.arange(tq, dtype=np.float32)
    for hd in range(DIFF_HEADS):
        a3 = _bf16_split3(a[hd])
        f3 = _bf16_split3(-(a[hd] * t_rel).astype(np.float32))
        for r in range(3):
            qx[hd, 0, r, :] = a3[r]
            qx[hd, 0, 3 + r, :] = f3[r]
    qx[:, 1] = -qx[:, 0]
    rel = s_rel[:, None] - t_rel[None, :]
    ndiag = tq // tkc
    diag = np.stack([np.stack([-(a[hd] * np.abs(rel + d * tkc)) for d in range(ndiag)])
                     for hd in range(DIFF_HEADS)]).astype(np.float32)
    return (jnp.asarray(a), jnp.asarray(kx, F32).astype(BF16), jnp.asarray(qx, F32).astype(BF16),
            jnp.asarray(diag))


def _diff_attn_kernel(a_ref, qt_ref, k_ref, vt_ref, kx_ref, qx_ref, diag_ref, lam_ref, sg_ref, o_ref,
                      qs_ref, m_ref, acc_ref, s_ref, *, tq, tkc, unroll):
    h = pl.program_id(1)
    i = pl.program_id(2)
    nk = k_ref.shape[1] // tkc
    ndiag = tq // tkc
    a_h = a_ref[h]

    qt = qt_ref[0]
    zero = jnp.zeros((HEAD_DIM, tq), BF16)
    for side in range(2):
        qs_ref[side, 0:HEAD_DIM, 0:tq] = qt[0:HEAD_DIM]
        qs_ref[side, 0:HEAD_DIM, tq:2 * tq] = zero
        qs_ref[side, HEAD_DIM:DIFF_V_DIM, 0:tq] = zero
        qs_ref[side, HEAD_DIM:DIFF_V_DIM, tq:2 * tq] = qt[HEAD_DIM:]
        qs_ref[side, DIFF_V_DIM:, 0:tq] = qx_ref[0, side]
        qs_ref[side, DIFF_V_DIM:, tq:2 * tq] = qx_ref[0, side]
    m_ref[...] = jnp.full(m_ref.shape, NEG_INF, F32)
    acc_ref[...] = jnp.zeros(acc_ref.shape, F32)

    def update(j, s, c):
        m_old = m_ref[...]
        m_new = jnp.maximum(m_old, jnp.max(s, axis=0, keepdims=True) + c)
        alpha = jnp.exp2(m_old - m_new)
        p = jnp.exp2(s - (m_new - c)).astype(BF16)
        acc_ref[...] = alpha * acc_ref[...] + jnp.dot(vt_ref[0, 0, j], p, preferred_element_type=F32)
        m_ref[...] = m_new

    n_left = i * ndiag
    for d in range(ndiag):
        j = n_left + d
        k0 = pl.multiple_of(j * tkc, tkc)
        s = jnp.dot(k_ref[0, pl.ds(k0, tkc), :], qs_ref[0, 0:DIFF_V_DIM, :], preferred_element_type=F32)
        bias = diag_ref[0, d]
        update(j, s + jnp.concatenate([bias, bias], axis=1), jnp.float32(0.0))

    cnt = nk - ndiag
    last = cnt - 1

    def chunk_of(n):
        n = jnp.minimum(n, last)
        side = (n >= n_left).astype(jnp.int32)
        j = n + side * ndiag
        dist = (1 - 2 * side) * (i * tq - j * tkc)
        return j, side, -a_h * dist.astype(F32)

    def scores(n, slot):
        j, side, _ = chunk_of(n)
        k0 = pl.multiple_of(j * tkc, tkc)
        lhs = jnp.concatenate([k_ref[0, pl.ds(k0, tkc), :], kx_ref[...]], axis=1)
        s_ref[slot] = jnp.dot(lhs, qs_ref[side], preferred_element_type=F32)

    def consume(n, slot):
        j, _, c = chunk_of(n)
        update(j, s_ref[slot], c)

    scores(0, 0)

    def group(g, carry):
        for u in range(unroll):
            scores(g * unroll + u + 1, (u + 1) % 2)
            consume(g * unroll + u, u % 2)
        return carry

    trips = cnt // unroll
    lax.fori_loop(0, trips, group, 0)
    for n in range(trips * unroll, cnt):
        if n + 1 < cnt:
            scores(n + 1, (n + 1) % 2)
        consume(n, n % 2)

    lam_v = lam_ref[...]
    lam = (jnp.exp(jnp.sum(lam_v[0:1] * lam_v[1:2], axis=1, keepdims=True))
           - jnp.exp(jnp.sum(lam_v[2:3] * lam_v[3:4], axis=1, keepdims=True)) + LAMBDA_INIT)
    acc = acc_ref[...]
    o = acc[0:DIFF_V_DIM] / acc[DIFF_V_DIM:DIFF_V_DIM + 1]
    o = o[:, 0:tq] - lam * o[:, tq:2 * tq]
    ms = jnp.mean(o * o, axis=0, keepdims=True)
    o = o * lax.rsqrt(ms + EPS) * sg_ref[...] * (1.0 - LAMBDA_INIT)
    o_ref[0] = o.T.astype(BF16)


def _diff_attn(qa_t, ka, va_t, lam_vecs, subln_g, tq=512, tkc=ATT_CHUNK, unroll=4):
    b, _, s = qa_t.shape
    nk = s // tkc
    a, kx, qx, diag = _alibi_tables(tq, tkc)
    kern = functools.partial(_diff_attn_kernel, tq=tq, tkc=tkc, unroll=unroll)
    c2 = lambda b_, h, i: (0, 0)
    return pl.pallas_call(
        kern,
        grid=(b, DIFF_HEADS, s // tq),
        in_specs=[pl.BlockSpec(memory_space=pltpu.SMEM),
                  pl.BlockSpec((1, DIFF_V_DIM, tq), lambda b_, h, i: (b_, h, i)),
                  pl.BlockSpec((1, s, DIFF_V_DIM), lambda b_, h, i: (b_, 0, h)),
                  pl.BlockSpec((1, 1, nk, VT_ROWS, tkc), lambda b_, h, i: (b_, h, 0, 0, 0)),
                  pl.BlockSpec((tkc, LANES), c2),
                  pl.BlockSpec((1, 2, LANES, tq), lambda b_, h, i: (h, 0, 0, 0)),
                  pl.BlockSpec((1, tq // tkc, tkc, tq), lambda b_, h, i: (h, 0, 0, 0)),
                  pl.BlockSpec((4, HEAD_DIM), c2),
                  pl.BlockSpec((DIFF_V_DIM, 1), c2)],
        out_specs=pl.BlockSpec((1, tq, DIFF_V_DIM), lambda b_, h, i: (b_, i, h)),
        out_shape=jax.ShapeDtypeStruct((b, s, A_W), BF16),
        scratch_shapes=[pltpu.VMEM((2, 2 * LANES, 2 * tq), BF16),
                        pltpu.VMEM((1, 2 * tq), F32),
                        pltpu.VMEM((VT_ROWS, 2 * tq), F32),
                        pltpu.VMEM((2, tkc, 2 * tq), F32)],
        compiler_params=_cparams(("parallel", "parallel", "arbitrary")),
        name="diff_attn",
    )(a, qa_t, ka, va_t, kx, qx, diag, lam_vecs, subln_g.reshape(DIFF_V_DIM, 1))


def _dil_attn_kernel(bias_ref, q_ref, kp_ref, kc_ref, kn_ref, vp_ref, vc_ref, vn_ref,
                     o_ref, lse_ref, *, n_sub, blk, qt):
    i = pl.program_id(2)
    nsb = qt // blk
    col = lax.broadcasted_iota(jnp.int32, (1, 3 * blk), 1)
    lane = lax.broadcasted_iota(jnp.int32, (blk, LANES), 1)
    first = lane < HEAD_DIM
    nt = (((1,), (1,)), ((), ()))

    def window(sb, prev_ref, cur_ref, next_ref, cols):
        parts = [prev_ref[0, :, cols] if sb == 0 else cur_ref[0, (sb - 1) * blk:sb * blk, cols],
                 cur_ref[0, sb * blk:(sb + 1) * blk, cols],
                 next_ref[0, :, cols] if sb == nsb - 1 else cur_ref[0, (sb + 1) * blk:(sb + 2) * blk, cols]]
        return jnp.concatenate(parts, axis=0)

    def scores(sb, hp):
        cols = slice(hp * LANES, (hp + 1) * LANES)
        q = q_ref[0, sb * blk:(sb + 1) * blk, cols]
        zq = jnp.zeros_like(q)
        q2 = jnp.concatenate([jnp.where(first, q, zq), jnp.where(first, zq, q)], axis=0)
        return lax.dot_general(q2, window(sb, kp_ref, kc_ref, kn_ref, cols), nt, preferred_element_type=F32)

    def finish(sb, hp, s):
        cols = slice(hp * LANES, (hp + 1) * LANES)
        s = s + jnp.concatenate([bias_ref[2 * hp], bias_ref[2 * hp + 1]], axis=0)
        if sb == 0 or sb == nsb - 1:
            kidx = i * qt + (sb - 1) * blk + col
            s = jnp.where((kidx >= 0) & (kidx < n_sub), s, NEG_INF)
        m = jnp.max(s, axis=1, keepdims=True)
        p = jnp.exp(s - m)
        l = jnp.sum(p, axis=1, keepdims=True)
        o = jnp.dot(p.astype(BF16), window(sb, vp_ref, vc_ref, vn_ref, cols), preferred_element_type=F32) / l
        lse = jnp.broadcast_to(m + jnp.log(l), (2 * blk, LANES))
        o_ref[0, sb * blk:(sb + 1) * blk, cols] = jnp.where(first, o[0:blk], o[blk:]).astype(BF16)
        lse_ref[0, sb * blk:(sb + 1) * blk, cols] = jnp.where(first, lse[0:blk], lse[blk:])

    units = [(sb, hp) for sb in range(nsb) for hp in range(DIL_HEADS // 2)]
    pending = scores(*units[0])
    for n, unit in enumerate(units):
        nxt = scores(*units[n + 1]) if n + 1 < len(units) else None
        finish(*unit, pending)
        pending = nxt


def _dil_attn(q2, k2, v2, window, dil, blk=128, qt=512):
    shape2 = q2.shape
    b, n_sub, w = shape2[0], shape2[1], shape2[2] // dil
    qt = min(qt, n_sub)
    nq = n_sub // qt
    nblk = n_sub // blk
    half = window // (2 * dil)
    slopes = 2.0 ** (-8.0 * np.arange(1, DIL_HEADS + 1) / DIL_HEADS)
    joff = np.arange(3 * blk)[None, :] - blk - np.arange(blk)[:, None]
    band = np.abs(joff) <= half
    bias = np.where(band[None], -slopes[:, None, None] * (np.abs(joff) * dil)[None], -np.inf).astype(np.float32)
    cur = lambda b_, r, i: (b_, i, r)
    prv = lambda b_, r, i: (b_, jnp.maximum(i * (qt // blk) - 1, 0), r)
    nxt = lambda b_, r, i: (b_, jnp.minimum((i + 1) * (qt // blk), nblk - 1), r)
    big = lambda f: pl.BlockSpec((1, qt, w), f)
    halo = lambda f: pl.BlockSpec((1, blk, w), f)
    kern = functools.partial(_dil_attn_kernel, n_sub=n_sub, blk=blk, qt=qt)
    o, lse = pl.pallas_call(
        kern,
        grid=(b, dil, nq),
        in_specs=[pl.BlockSpec((DIL_HEADS, blk, 3 * blk), lambda b_, r, i: (0, 0, 0)),
                  big(cur), halo(prv), big(cur), halo(nxt), halo(prv), big(cur), halo(nxt)],
        out_specs=[big(cur), big(cur)],
        out_shape=[jax.ShapeDtypeStruct(shape2, BF16), jax.ShapeDtypeStruct(shape2, F32)],
        compiler_params=_cparams(("parallel", "parallel", "parallel")),
        name=f"dil_attn_d{dil}",
    )(jnp.asarray(bias), q2, k2, k2, k2, v2, v2, v2)
    return o, lse


def _out_proj_kernel(x_ref, oa_ref, o1_ref, o2_ref, o3_ref, l1_ref, l2_ref, l3_ref,
                     wa_ref, wb_ref, g_ref, x1_ref, xn_ref, slab_ref):
    tm = x_ref.shape[1]
    nsl = B_W // LANES

    def natural(ref, dil, base):
        if dil == 1:
            return ref[0].astype(F32)
        for r in range(dil):
            for sl in range(nsl):
                slab_ref[base + sl, pl.ds(r, tm // dil, stride=dil), :] = (
                    ref[0, :, r * B_W + sl * LANES:r * B_W + (sl + 1) * LANES].astype(F32))
        return jnp.concatenate([slab_ref[base + sl] for sl in range(nsl)], axis=1)

    dils = [dil for _, dil in DIL_PATTERNS]
    l1, l2, l3 = [natural(ref, dil, (2 * n) * nsl) for n, (ref, dil) in enumerate(zip((l1_ref, l2_ref, l3_ref), dils))]
    o1, o2, o3 = [natural(ref, dil, (2 * n + 1) * nsl) for n, (ref, dil) in enumerate(zip((o1_ref, o2_ref, o3_ref), dils))]
    m = jnp.maximum(jnp.maximum(l1, l2), l3)
    e1, e2, e3 = jnp.exp(l1 - m), jnp.exp(l2 - m), jnp.exp(l3 - m)
    ob = (e1 * o1 + e2 * o2 + e3 * o3) / (e1 + e2 + e3)
    y = (x_ref[0]
         + jnp.dot(oa_ref[0], wa_ref[...], preferred_element_type=F32)
         + jnp.dot(ob.astype(BF16), wb_ref[...], preferred_element_type=F32))
    x1_ref[0] = y
    xn_ref[0] = _rms(y, g_ref[...]).astype(BF16)


def _out_proj(x, oa, obs, lses, w_out, g, tm=512):
    b, s, d = x.shape
    wa = w_out[0:A_W].astype(BF16)
    wb = w_out[A_W:].astype(BF16)
    tok = lambda i, j: (i, j, 0)
    const = lambda i, j: (0, 0)
    dil_specs = [pl.BlockSpec((1, tm // dil, dil * B_W), tok) for _, dil in DIL_PATTERNS]
    return pl.pallas_call(
        _out_proj_kernel,
        grid=(b, s // tm),
        in_specs=[pl.BlockSpec((1, tm, d), tok), pl.BlockSpec((1, tm, A_W), tok)] + dil_specs + dil_specs +
                 [pl.BlockSpec((A_W, d), const), pl.BlockSpec((B_W, d), const), pl.BlockSpec((1, d), const)],
        out_specs=[pl.BlockSpec((1, tm, d), tok), pl.BlockSpec((1, tm, d), tok)],
        out_shape=[jax.ShapeDtypeStruct((b, s, d), F32), jax.ShapeDtypeStruct((b, s, d), BF16)],
        scratch_shapes=[pltpu.VMEM((2 * len(DIL_PATTERNS) * B_W // LANES, tm, LANES), F32)],
        compiler_params=_cparams(("parallel", "parallel")),
        name="out_proj",
    )(x, oa, *obs, *lses, wa, wb, g.reshape(1, d))


W_ROWS = N_KEYS // 2
I1_GROUP = 8
PEER_N1 = 16


def _cand_tables():
    groups = [(0, 16), (1, 8)] + [(a, 8) for a in range(2, 8)]
    a_idx, b_idx = [], []
    for a, n in groups:
        a_idx += [a] * n
        b_idx += list(range(n))
    a_idx += list(range(8, 16))
    b_idx += [0] * 8
    a_idx, b_idx = np.array(a_idx), np.array(b_idx)
    ok = (a_idx + 1) * (b_idx + 1) <= PEER_TOPK
    pos = np.where(ok, a_idx * PEER_TOPK + b_idx, -1)
    return groups, pos.astype(np.int32)


def _top_rows(s, k):
    n = s.shape[0]
    iota = lax.broadcasted_iota(jnp.int32, s.shape, 0)
    vals, idxs = [], []
    for _ in range(k):
        m = jnp.max(s, axis=0, keepdims=True)
        ix = jnp.min(jnp.where(s == m, iota, n), axis=0, keepdims=True)
        vals.append(m)
        idxs.append(ix)
        s = jnp.where(iota == ix, NEG_INF, s)
    return jnp.concatenate(vals, axis=0), jnp.concatenate(idxs, axis=0)


def _peer_select_kernel(xn_ref, wq_t_ref, keys_ref, pos_ref, w_ref,
                        q_ref, v_ref, ix_ref, e_ref, gt_ref, i1_ref, i2_ref, g_ref, *, groups):
    nt = (((1,), (1,)), ((), ()))
    tt = xn_ref.shape[0]

    @pl.when(pl.program_id(0) == 0)
    def _no_previous_tile():
        for ref in (i1_ref, i2_ref, g_ref):
            ref[...] = jnp.zeros(ref.shape, F32)

    q_ref[...] = lax.dot_general(wq_t_ref[...], xn_ref[...], nt,
                                 preferred_element_type=F32).astype(BF16)

    row = lax.broadcasted_iota(jnp.int32, (N_KEYS, N_KEYS), 0)
    pr = row & (W_ROWS - 1)
    half = I1_GROUP // 2
    i1_of_row = ((pr // half) * I1_GROUP + (pr % half) + half * (row // W_ROWS)).astype(F32)
    sub = row.astype(F32)
    hi_mask = jnp.uint32(0xFFFF0000)
    nch, rpc = w_ref.shape[0], W_ROWS // w_ref.shape[0]

    def gate_words(t):
        r1 = i1_ref[pl.ds(t, 1), :]
        r2 = i2_ref[pl.ds(t, 1), :]
        gg = g_ref[pl.ds(t, 1), :]
        lt = jnp.where(i1_of_row == r1, 1.0, 0.0).astype(BF16)
        rt = jnp.where(sub == r2, gg, 0.0).astype(BF16)
        w = lax.dot_general(lt, rt, nt, preferred_element_type=F32)
        bits = lax.bitcast_convert_type(w.astype(BF16).astype(F32), jnp.uint32)
        words = (bits[W_ROWS:] & hi_mask) | (bits[0:W_ROWS] >> 16)
        w_ref[:, pl.ds(pl.multiple_of(t * rpc, rpc), rpc), :] = words.reshape(nch, rpc, N_KEYS)

    per_half = tt // (2 * PEER_HEADS)

    def half_topk(hc, carry):
        qh = q_ref[pl.ds(pl.multiple_of(hc * PEER_HALF, PEER_HALF), PEER_HALF), :]
        s = jnp.dot(keys_ref[hc % 2], qh, preferred_element_type=F32)
        for u in range(per_half):
            gate_words(hc * per_half + u)
        vals, idxs = _top_rows(s, PEER_TOPK)
        v_ref[hc] = vals
        ix_ref[hc] = idxs
        return carry

    lax.fori_loop(0, 2 * PEER_HEADS, half_topk, 0, unroll=2)

    pos_tab = pos_ref[...]
    big = jnp.int32(1 << 30)

    def head_select(h, carry):
        v1, v2 = v_ref[2 * h], v_ref[2 * h + 1]
        i1, i2 = ix_ref[2 * h], ix_ref[2 * h + 1]
        v1g, v2g, i1g, i2g = [], [], [], []
        for a, n in groups:
            v1g.append(jnp.broadcast_to(v1[a:a + 1], (n, tt)))
            i1g.append(jnp.broadcast_to(i1[a:a + 1], (n, tt)))
            v2g.append(v2[0:n])
            i2g.append(i2[0:n])
        v1g.append(v1[8:16]); i1g.append(i1[8:16])
        v2g.append(jnp.broadcast_to(v2[0:1], (8, tt))); i2g.append(jnp.broadcast_to(i2[0:1], (8, tt)))
        cand = jnp.concatenate(v1g, axis=0) + jnp.concatenate(v2g, axis=0)
        eidx = jnp.concatenate(i1g, axis=0) * N_KEYS + jnp.concatenate(i2g, axis=0)
        cand = jnp.where(pos_tab >= 0, cand, NEG_INF)
        key = pos_tab * (N_KEYS * N_KEYS) + eidx
        tops, sel = [], []
        for _ in range(PEER_TOPK):
            m = jnp.max(cand, axis=0, keepdims=True)
            kk = jnp.min(jnp.where(cand == m, key, big), axis=0, keepdims=True)
            tops.append(m)
            sel.append(kk)
            cand = jnp.where(key == kk, NEG_INF, cand)
        top = jnp.concatenate(tops, axis=0)
        ek = jnp.concatenate(sel, axis=0) & (N_KEYS * N_KEYS - 1)
        ex = jnp.exp(top - top[0:1])
        gate = ex / jnp.sum(ex, axis=0, keepdims=True)
        e_ref[h] = ek.astype(F32)
        gt_ref[h] = gate
        return carry

    lax.fori_loop(0, PEER_HEADS, head_select, 0, unroll=2)

    ef = e_ref[...].reshape(PEER_HEADS * PEER_TOPK, tt)
    i1f = jnp.floor(ef * (1.0 / N_KEYS))
    i1_ref[...] = i1f.T
    i2_ref[...] = (ef - i1f * N_KEYS).T
    g_ref[...] = gt_ref[...].reshape(PEER_HEADS * PEER_TOPK, tt).T


def _peer_select(xn, wq, keys1, keys2, n1, tt=256):
    t, d = xn.shape
    groups, pos = _cand_tables()
    pos_tab = jnp.broadcast_to(jnp.asarray(pos)[:, None], (pos.shape[0], tt))
    nslot = PEER_HEADS * PEER_TOPK
    qw = 2 * PEER_HEADS * PEER_HALF
    nch = N_KEYS // n1
    rpc = W_ROWS // nch
    const = lambda i: (0, 0)
    kern = functools.partial(_peer_select_kernel, groups=groups)
    ntile = t // tt
    return pl.pallas_call(
        kern,
        grid=(ntile + 1,),
        in_specs=[pl.BlockSpec((tt, d), lambda i: (jnp.minimum(i, ntile - 1), 0)),
                  pl.BlockSpec((qw, d), const),
                  pl.BlockSpec((2, N_KEYS, PEER_HALF), lambda i: (0, 0, 0)),
                  pl.BlockSpec((pos.shape[0], tt), const)],
        out_specs=pl.BlockSpec((nch, tt * rpc, N_KEYS), lambda i: (0, jnp.maximum(i - 1, 0), 0)),
        out_shape=jax.ShapeDtypeStruct((nch, t * rpc, N_KEYS), jnp.uint32),
        scratch_shapes=[pltpu.VMEM((qw, tt), BF16),
                        pltpu.VMEM((2 * PEER_HEADS, PEER_TOPK, tt), F32),
                        pltpu.VMEM((2 * PEER_HEADS, PEER_TOPK, tt), jnp.int32),
                        pltpu.VMEM((PEER_HEADS, PEER_TOPK, tt), F32),
                        pltpu.VMEM((PEER_HEADS, PEER_TOPK, tt), F32),
                        pltpu.VMEM((tt, nslot), F32),
                        pltpu.VMEM((tt, nslot), F32),
                        pltpu.VMEM((tt, nslot), F32)],
        compiler_params=_cparams(("arbitrary",)),
        name="peer_select",
    )(xn, wq.T.astype(BF16), jnp.stack([keys1, keys2]).astype(BF16), pos_tab)


def _gelu(x):
    c = math.sqrt(2.0 / math.pi)
    return 0.5 * x * (1.0 + jnp.tanh(c * (x + 0.044715 * (x * x * x))))


def _peer_experts_kernel(x1_ref, xn_ref, w_ref, u_ref, v_ref, fg_ref, y_ref, acc_ref, *, tt, n1):
    c = pl.program_id(1)
    nc = pl.num_programs(1)
    nt = (((1,), (1,)), ((), ()))
    hi_mask = jnp.uint32(0xFFFF0000)
    half = I1_GROUP // 2
    rpc = n1 // 2

    @pl.when(c == 0)
    def _init():
        acc_ref[...] = jnp.zeros(acc_ref.shape, F32)

    a = lax.dot_general(xn_ref[...], u_ref[...], nt, preferred_element_type=F32)
    lo, hi = [], []
    for g in range(n1 // I1_GROUP):
        for u in range(half):
            words = w_ref[0, pl.ds(g * half + u, tt, stride=rpc), :]
            lo.append(lax.bitcast_convert_type(words << 16, F32))
            hi.append(lax.bitcast_convert_type(words & hi_mask, F32))
    wc = jnp.concatenate(
        [part[g * half + u] for g in range(n1 // I1_GROUP) for part in (lo, hi) for u in range(half)], axis=1)
    z = (_gelu(a) * wc).astype(BF16)
    acc_ref[...] += jnp.dot(z, v_ref[...], preferred_element_type=F32)

    @pl.when(c == nc - 1)
    def _fin():
        y_ref[...] = _rms(x1_ref[...] + acc_ref[...], fg_ref[...])


def _peer_experts(x1, xn, words, u_tab, v_tab, final_g, n1, tt=512):
    t, d = x1.shape
    ne = u_tab.shape[0]
    ec = n1 * N_KEYS
    rpc = n1 // 2
    tok = lambda i, c: (i, 0)
    kern = functools.partial(_peer_experts_kernel, tt=tt, n1=n1)
    return pl.pallas_call(
        kern,
        grid=(t // tt, ne // ec),
        in_specs=[pl.BlockSpec((tt, d), tok), pl.BlockSpec((tt, d), tok),
                  pl.BlockSpec((1, tt * rpc, N_KEYS), lambda i, c: (c, i, 0)),
                  pl.BlockSpec((ec, d), lambda i, c: (c, 0)),
                  pl.BlockSpec((ec, d), lambda i, c: (c, 0)),
                  pl.BlockSpec((1, d), lambda i, c: (0, 0))],
        out_specs=pl.BlockSpec((tt, d), tok),
        out_shape=jax.ShapeDtypeStruct((t, d), F32),
        scratch_shapes=[pltpu.VMEM((tt, d), F32)],
        compiler_params=_cparams(("parallel", "arbitrary")),
        name="peer_experts",
    )(x1, xn, words, u_tab, v_tab, final_g.reshape(1, d))


def _trunk(x, norm1_g, w_in, lam_vecs, subln_g, w_out, norm2_g, wq, keys1, keys2, u_bf, v_bf, final_g):
    b, s, d = x.shape
    qa_t, ka, va_t, qkv = _in_proj(x, norm1_g, w_in)
    oa = _diff_attn(qa_t, ka, va_t, lam_vecs, subln_g)
    obs, lses = [], []
    for (window, dil), (q2, k2, v2) in zip(DIL_PATTERNS, qkv):
        o, lse = _dil_attn(q2, k2, v2, window, dil)
        obs.append(o)
        lses.append(lse)
    x1, xn2 = _out_proj(x, oa, obs, lses, w_out, norm2_g)
    x1, xn2 = x1.reshape(b * s, d), xn2.reshape(b * s, d)
    words = _peer_select(xn2, wq, keys1, keys2, PEER_N1)
    y = _peer_experts(x1, xn2, words, u_bf, v_bf, final_g, PEER_N1)
    return y.reshape(b, s, d)


def kernel(x_prompt, x_sample, norm1_g, w_in, lambda_q1, lambda_k1, lambda_q2, lambda_k2, subln_g, w_out,
           norm2_g, peer_wq, peer_keys1, peer_keys2, peer_u, peer_v, final_g):
    lam_vecs = jnp.concatenate([lambda_q1, lambda_k1, lambda_q2, lambda_k2], axis=0).astype(F32)
    u_bf = peer_u[0].astype(BF16)
    v_bf = peer_v[0].astype(BF16)
    args = (norm1_g[0], w_in[0], lam_vecs, subln_g[0], w_out[0], norm2_g[0], peer_wq[0],
            peer_keys1[0], peer_keys2[0], u_bf, v_bf, final_g)
    return (_trunk(x_prompt, *args), _trunk(x_sample, *args))
```

```python
import functools
import math

import jax
import jax.numpy as jnp
import numpy as np
from jax import lax
from jax.experimental import pallas as pl
from jax.experimental.pallas import tpu as pltpu

D_MODEL = 1024
HEAD_DIM = 64
DIFF_HEADS = 4
DIFF_V_DIM = 2 * HEAD_DIM
DIL_HEADS = 8
DIL_PATTERNS = ((128, 1), (512, 4), (2048, 16))
A_W = DIFF_HEADS * 2 * HEAD_DIM
B_W = DIL_HEADS * HEAD_DIM
PEER_HEADS = 8
N_KEYS = 128
PEER_HALF = 128
PEER_TOPK = 16
EPS = 1e-6
LAMBDA_INIT = 0.8 - 0.6 * math.exp(-0.3 * 0)

LOG2E = math.log2(math.e)
ATT_CHUNK = 256
VT_ROWS = DIFF_V_DIM + 16
LANES = 128
VMEM_LIMIT = 56 * 1024 * 1024

F32 = jnp.float32
BF16 = jnp.bfloat16
NEG_INF = float("-inf")


def _cparams(sem):
    return pltpu.CompilerParams(dimension_semantics=sem, vmem_limit_bytes=VMEM_LIMIT)


def _unrolled_loop(lo, hi, body, unroll):
    trips = (hi - lo) // unroll

    def group(g, carry):
        for u in range(unroll):
            body(lo + g * unroll + u, carry)
        return carry

    lax.fori_loop(0, trips, group, 0)
    lax.fori_loop(lo + trips * unroll, hi, body, 0)


def _rms(x, g):
    return x * lax.rsqrt(jnp.mean(x * x, axis=-1, keepdims=True) + EPS) * g


def _in_proj_kernel(x_ref, g_ref, wqa_t_ref, wva_t_ref, wrest_ref, qa_t_ref, ka_ref, va_t_ref, *rest_refs, tkc):
    dil_refs, slab_ref = rest_refs[:-1], rest_refs[-1]
    xn = _rms(x_ref[0], g_ref[...]).astype(BF16)
    tm = xn.shape[0]
    nt = (((1,), (1,)), ((), ()))
    qa_t_ref[0] = lax.dot_general(wqa_t_ref[...], xn, nt, preferred_element_type=F32).astype(BF16)
    va_t = lax.dot_general(wva_t_ref[...], xn, nt, preferred_element_type=F32).astype(BF16)
    ones = jnp.ones((VT_ROWS - DIFF_V_DIM, tkc), BF16)
    for hd in range(DIFF_HEADS):
        for c in range(tm // tkc):
            va_t_ref[0, hd, c, 0:DIFF_V_DIM, :] = va_t[hd * DIFF_V_DIM:(hd + 1) * DIFF_V_DIM, c * tkc:(c + 1) * tkc]
            va_t_ref[0, hd, c, DIFF_V_DIM:VT_ROWS, :] = ones
    rest = jnp.dot(xn, wrest_ref[...], preferred_element_type=F32)
    ka_ref[0] = rest[:, 0:A_W].astype(BF16)
    nsl = B_W // LANES
    for a in range(3):
        cols = rest[:, (1 + a) * B_W:(2 + a) * B_W]
        dil_refs[a][0] = cols.astype(BF16)
        for sl in range(nsl):
            slab_ref[a * nsl + sl] = cols[:, sl * LANES:(sl + 1) * LANES]
    for pi, (_, dil) in enumerate(DIL_PATTERNS[1:]):
        for a in range(3):
            out = dil_refs[3 * (pi + 1) + a]
            for r in range(dil):
                for sl in range(nsl):
                    rows = slab_ref[a * nsl + sl, pl.ds(r, tm // dil, stride=dil), :]
                    out[0, :, r * B_W + sl * LANES:r * B_W + (sl + 1) * LANES] = rows.astype(BF16)


def _in_proj(x, g, w_in, tm=512, tkc=ATT_CHUNK):
    b, s, d = x.shape
    assert DIL_PATTERNS[0][1] == 1
    scale = HEAD_DIM ** -0.5
    wqa_t = (w_in[:, 0:A_W] * (scale * LOG2E)).T.astype(BF16)
    wva_t = w_in[:, 2 * A_W:3 * A_W].T.astype(BF16)
    wrest = jnp.concatenate(
        [w_in[:, A_W:2 * A_W], w_in[:, 3 * A_W:4 * A_W] * scale, w_in[:, 4 * A_W:]], axis=1).astype(BF16)
    tok = lambda i, j: (i, j, 0)
    tr = lambda i, j: (i, 0, j)
    const = lambda i, j: (0, 0)
    dil_specs, dil_shapes = [], []
    for _, dil in DIL_PATTERNS:
        dil_specs += [pl.BlockSpec((1, tm // dil, dil * B_W), tok)] * 3
        dil_shapes += [jax.ShapeDtypeStruct((b, s // dil, dil * B_W), BF16)] * 3
    outs = pl.pallas_call(
        functools.partial(_in_proj_kernel, tkc=tkc),
        grid=(b, s // tm),
        in_specs=[pl.BlockSpec((1, tm, d), tok),
                  pl.BlockSpec((1, d), const),
                  pl.BlockSpec((A_W, d), const),
                  pl.BlockSpec((A_W, d), const),
                  pl.BlockSpec((d, 4 * A_W), const)],
        out_specs=[pl.BlockSpec((1, A_W, tm), tr),
                   pl.BlockSpec((1, tm, A_W), tok),
                   pl.BlockSpec((1, DIFF_HEADS, tm // tkc, VT_ROWS, tkc), lambda i, j: (i, 0, j, 0, 0))] + dil_specs,
        out_shape=[jax.ShapeDtypeStruct((b, A_W, s), BF16),
                   jax.ShapeDtypeStruct((b, s, A_W), BF16),
                   jax.ShapeDtypeStruct((b, DIFF_HEADS, s // tkc, VT_ROWS, tkc), BF16)] + dil_shapes,
        scratch_shapes=[pltpu.VMEM((3 * B_W // LANES, tm, LANES), F32)],
        compiler_params=_cparams(("parallel", "parallel")),
        name="in_proj",
    )(x, g.reshape(1, d), wqa_t, wva_t, wrest)
    qkv = [tuple(outs[3 + 3 * p:6 + 3 * p]) for p in range(len(DIL_PATTERNS))]
    return outs[0], outs[1], outs[2], qkv


def _bf16_split3(x):
    x = np.asarray(x, np.float32)
    pieces = []
    for _ in range(3):
        p = x.astype(jnp.bfloat16).astype(np.float32)
        pieces.append(p.astype(np.float64))
        x = (x - p).astype(np.float32)
    return pieces


def _alibi_tables(tq, tkc):
    slopes = 2.0 ** (-8.0 * np.arange(1, DIFF_HEADS + 1) / DIFF_HEADS)
    a = (slopes * LOG2E).astype(np.float32)
    kx = np.zeros((tkc, LANES), np.float32)
    s_rel = np.arange(tkc, dtype=np.float32)
    kx[:, 0:3] = s_rel[:, None]
    kx[:, 3:6] = 1.0
    qx = np.zeros((DIFF_HEADS, 2, LANES, tq), np.float64)
    t_rel = np.arange(tq, dtype=np.float32)
    for hd in range(DIFF_HEADS):
        a3 = _bf16_split3(a[hd])
        f3 = _bf16_split3(-(a[hd] * t_rel).astype(np.float32))
        for r in range(3):
            qx[hd, 0, r, :] = a3[r]
            qx[hd, 0, 3 + r, :] = f3[r]
    qx[:, 1] = -qx[:, 0]
    rel = s_rel[:, None] - t_rel[None, :]
    ndiag = tq // tkc
    diag = np.stack([np.stack([-(a[hd] * np.abs(rel + d * tkc)) for d in range(ndiag)])
                     for hd in range(DIFF_HEADS)]).astype(np.float32)
    return (jnp.asarray(a), jnp.asarray(kx, F32).astype(BF16), jnp.asarray(qx, F32).astype(BF16),
            jnp.asarray(diag))


def _diff_attn_kernel(a_ref, qt_ref, k_ref, vt_ref, kx_ref, qx_ref, diag_ref, lam_ref, sg_ref, o_ref,
                      qs_ref, m_ref, acc_ref, s_ref, *, tq, tkc, unroll):
    h = pl.program_id(1)
    i = pl.program_id(2)
    nk = k_ref.shape[1] // tkc
    ndiag = tq // tkc
    a_h = a_ref[h]

    qt = qt_ref[0]
    zero = jnp.zeros((HEAD_DIM, tq), BF16)
    for side in range(2):
        qs_ref[side, 0:HEAD_DIM, 0:tq] = qt[0:HEAD_DIM]
        qs_ref[side, 0:HEAD_DIM, tq:2 * tq] = zero
        qs_ref[side, HEAD_DIM:DIFF_V_DIM, 0:tq] = zero
        qs_ref[side, HEAD_DIM:DIFF_V_DIM, tq:2 * tq] = qt[HEAD_DIM:]
        qs_ref[side, DIFF_V_DIM:, 0:tq] = qx_ref[0, side]
        qs_ref[side, DIFF_V_DIM:, tq:2 * tq] = qx_ref[0, side]
    m_ref[...] = jnp.full(m_ref.shape, NEG_INF, F32)
    acc_ref[...] = jnp.zeros(acc_ref.shape, F32)

    def update(j, s, c):
        m_old = m_ref[...]
        m_new = jnp.maximum(m_old, jnp.max(s, axis=0, keepdims=True) + c)
        alpha = jnp.exp2(m_old - m_new)
        p = jnp.exp2(s - (m_new - c)).astype(BF16)
        acc_ref[...] = alpha * acc_ref[...] + jnp.dot(vt_ref[0, 0, j], p, preferred_element_type=F32)
        m_ref[...] = m_new

    n_left = i * ndiag
    for d in range(ndiag):
        j = n_left + d
        k0 = pl.multiple_of(j * tkc, tkc)
        s = jnp.dot(k_ref[0, pl.ds(k0, tkc), :], qs_ref[0, 0:DIFF_V_DIM, :], preferred_element_type=F32)
        bias = diag_ref[0, d]
        update(j, s + jnp.concatenate([bias, bias], axis=1), jnp.float32(0.0))

    cnt = nk - ndiag
    last = cnt - 1

    def chunk_of(n):
        n = jnp.minimum(n, last)
        side = (n >= n_left).astype(jnp.int32)
        j = n + side * ndiag
        dist = (1 - 2 * side) * (i * tq - j * tkc)
        return j, side, -a_h * dist.astype(F32)

    def scores(n, slot):
        j, side, _ = chunk_of(n)
        k0 = pl.multiple_of(j * tkc, tkc)
        lhs = jnp.concatenate([k_ref[0, pl.ds(k0, tkc), :], kx_ref[...]], axis=1)
        s_ref[slot] = jnp.dot(lhs, qs_ref[side], preferred_element_type=F32)

    def consume(n, slot):
        j, _, c = chunk_of(n)
        update(j, s_ref[slot], c)

    scores(0, 0)

    def group(g, carry):
        for u in range(unroll):
            scores(g * unroll + u + 1, (u + 1) % 2)
            consume(g * unroll + u, u % 2)
        return carry

    trips = cnt // unroll
    lax.fori_loop(0, trips, group, 0)
    for n in range(trips * unroll, cnt):
        if n + 1 < cnt:
            scores(n + 1, (n + 1) % 2)
        consume(n, n % 2)

    lam_v = lam_ref[...]
    lam = (jnp.exp(jnp.sum(lam_v[0:1] * lam_v[1:2], axis=1, keepdims=True))
           - jnp.exp(jnp.sum(lam_v[2:3] * lam_v[3:4], axis=1, keepdims=True)) + LAMBDA_INIT)
    acc = acc_ref[...]
    o = acc[0:DIFF_V_DIM] / acc[DIFF_V_DIM:DIFF_V_DIM + 1]
    o = o[:, 0:tq] - lam * o[:, tq:2 * tq]
    ms = jnp.mean(o * o, axis=0, keepdims=True)
    o = o * lax.rsqrt(ms + EPS) * sg_ref[...] * (1.0 - LAMBDA_INIT)
    o_ref[0] = o.T.astype(BF16)


def _diff_attn(qa_t, ka, va_t, lam_vecs, subln_g, tq=512, tkc=ATT_CHUNK, unroll=4):
    b, _, s = qa_t.shape
    nk = s // tkc
    a, kx, qx, diag = _alibi_tables(tq, tkc)
    kern = functools.partial(_diff_attn_kernel, tq=tq, tkc=tkc, unroll=unroll)
    c2 = lambda b_, h, i: (0, 0)
    return pl.pallas_call(
        kern,
        grid=(b, DIFF_HEADS, s // tq),
        in_specs=[pl.BlockSpec(memory_space=pltpu.SMEM),
                  pl.BlockSpec((1, DIFF_V_DIM, tq), lambda b_, h, i: (b_, h, i)),
                  pl.BlockSpec((1, s, DIFF_V_DIM), lambda b_, h, i: (b_, 0, h)),
                  pl.BlockSpec((1, 1, nk, VT_ROWS, tkc), lambda b_, h, i: (b_, h, 0, 0, 0)),
                  pl.BlockSpec((tkc, LANES), c2),
                  pl.BlockSpec((1, 2, LANES, tq), lambda b_, h, i: (h, 0, 0, 0)),
                  pl.BlockSpec((1, tq // tkc, tkc, tq), lambda b_, h, i: (h, 0, 0, 0)),
                  pl.BlockSpec((4, HEAD_DIM), c2),
                  pl.BlockSpec((DIFF_V_DIM, 1), c2)],
        out_specs=pl.BlockSpec((1, tq, DIFF_V_DIM), lambda b_, h, i: (b_, i, h)),
        out_shape=jax.ShapeDtypeStruct((b, s, A_W), BF16),
        scratch_shapes=[pltpu.VMEM((2, 2 * LANES, 2 * tq), BF16),
                        pltpu.VMEM((1, 2 * tq), F32),
                        pltpu.VMEM((VT_ROWS, 2 * tq), F32),
                        pltpu.VMEM((2, tkc, 2 * tq), F32)],
        compiler_params=_cparams(("parallel", "parallel", "arbitrary")),
        name="diff_attn",
    )(a, qa_t, ka, va_t, kx, qx, diag, lam_vecs, subln_g.reshape(DIFF_V_DIM, 1))


def _dil_attn_kernel(bias_ref, q_ref, kp_ref, kc_ref, kn_ref, vp_ref, vc_ref, vn_ref,
                     o_ref, lse_ref, s_ref, *, n_sub, blk, qt):
    i = pl.program_id(2)
    nsb = qt // blk
    col = lax.broadcasted_iota(jnp.int32, (1, 3 * blk), 1)
    lane = lax.broadcasted_iota(jnp.int32, (blk, LANES), 1)
    first = lane < HEAD_DIM
    nt = (((1,), (1,)), ((), ()))

    def window(sb, prev_ref, cur_ref, next_ref, cols):
        parts = [prev_ref[0, :, cols] if sb == 0 else cur_ref[0, (sb - 1) * blk:sb * blk, cols],
                 cur_ref[0, sb * blk:(sb + 1) * blk, cols],
                 next_ref[0, :, cols] if sb == nsb - 1 else cur_ref[0, (sb + 1) * blk:(sb + 2) * blk, cols]]
        return jnp.concatenate(parts, axis=0)

    def scores(sb, hp, slot):
        cols = slice(hp * LANES, (hp + 1) * LANES)
        q = q_ref[0, sb * blk:(sb + 1) * blk, cols]
        zq = jnp.zeros_like(q)
        q2 = jnp.concatenate([jnp.where(first, q, zq), jnp.where(first, zq, q)], axis=0)
        s_ref[slot] = lax.dot_general(q2, window(sb, kp_ref, kc_ref, kn_ref, cols), nt,
                                      preferred_element_type=F32)

    def finish(sb, hp, slot):
        cols = slice(hp * LANES, (hp + 1) * LANES)
        s = s_ref[slot] + jnp.concatenate([bias_ref[2 * hp], bias_ref[2 * hp + 1]], axis=0)
        if sb == 0 or sb == nsb - 1:
            kidx = i * qt + (sb - 1) * blk + col
            s = jnp.where((kidx >= 0) & (kidx < n_sub), s, NEG_INF)
        m = jnp.max(s, axis=1, keepdims=True)
        p = jnp.exp(s - m)
        l = jnp.sum(p, axis=1, keepdims=True)
        o = jnp.dot(p.astype(BF16), window(sb, vp_ref, vc_ref, vn_ref, cols), preferred_element_type=F32) / l
        lse = jnp.broadcast_to(m + jnp.log(l), (2 * blk, LANES))
        o_ref[0, sb * blk:(sb + 1) * blk, cols] = jnp.where(first, o[0:blk], o[blk:]).astype(BF16)
        lse_ref[0, sb * blk:(sb + 1) * blk, cols] = jnp.where(first, lse[0:blk], lse[blk:])

    units = [(sb, hp) for sb in range(nsb) for hp in range(DIL_HEADS // 2)]
    nslot = s_ref.shape[0]
    for n in range(min(nslot - 1, len(units))):
        scores(*units[n], n % nslot)
    for n, unit in enumerate(units):
        if n + nslot - 1 < len(units):
            scores(*units[n + nslot - 1], (n + nslot - 1) % nslot)
        finish(*unit, n % nslot)


def _dil_attn(q2, k2, v2, window, dil, blk=128, qt=512):
    shape2 = q2.shape
    b, n_sub, w = shape2[0], shape2[1], shape2[2] // dil
    qt = min(qt, n_sub)
    nq = n_sub // qt
    nblk = n_sub // blk
    half = window // (2 * dil)
    slopes = 2.0 ** (-8.0 * np.arange(1, DIL_HEADS + 1) / DIL_HEADS)
    joff = np.arange(3 * blk)[None, :] - blk - np.arange(blk)[:, None]
    band = np.abs(joff) <= half
    bias = np.where(band[None], -slopes[:, None, None] * (np.abs(joff) * dil)[None], -np.inf).astype(np.float32)
    cur = lambda b_, r, i: (b_, i, r)
    prv = lambda b_, r, i: (b_, jnp.maximum(i * (qt // blk) - 1, 0), r)
    nxt = lambda b_, r, i: (b_, jnp.minimum((i + 1) * (qt // blk), nblk - 1), r)
    big = lambda f: pl.BlockSpec((1, qt, w), f)
    halo = lambda f: pl.BlockSpec((1, blk, w), f)
    kern = functools.partial(_dil_attn_kernel, n_sub=n_sub, blk=blk, qt=qt)
    o, lse = pl.pallas_call(
        kern,
        grid=(b, dil, nq),
        in_specs=[pl.BlockSpec((DIL_HEADS, blk, 3 * blk), lambda b_, r, i: (0, 0, 0)),
                  big(cur), halo(prv), big(cur), halo(nxt), halo(prv), big(cur), halo(nxt)],
        out_specs=[big(cur), big(cur)],
        out_shape=[jax.ShapeDtypeStruct(shape2, BF16), jax.ShapeDtypeStruct(shape2, F32)],
        scratch_shapes=[pltpu.VMEM((3, 2 * blk, 3 * blk), F32)],
        compiler_params=_cparams(("parallel", "parallel", "parallel")),
        name=f"dil_attn_d{dil}",
    )(jnp.asarray(bias), q2, k2, k2, k2, v2, v2, v2)
    return o, lse


def _out_proj_kernel(x_ref, oa_ref, o1_ref, o2_ref, o3_ref, l1_ref, l2_ref, l3_ref,
                     wa_ref, wb_ref, g_ref, x1_ref, xn_ref, slab_ref):
    tm = x_ref.shape[1]
    nsl = B_W // LANES

    def natural(ref, dil, base):
        if dil == 1:
            return ref[0].astype(F32)
        for r in range(dil):
            for sl in range(nsl):
                slab_ref[base + sl, pl.ds(r, tm // dil, stride=dil), :] = (
                    ref[0, :, r * B_W + sl * LANES:r * B_W + (sl + 1) * LANES].astype(F32))
        return jnp.concatenate([slab_ref[base + sl] for sl in range(nsl)], axis=1)

    dils = [dil for _, dil in DIL_PATTERNS]
    l1, l2, l3 = [natural(ref, dil, (2 * n) * nsl) for n, (ref, dil) in enumerate(zip((l1_ref, l2_ref, l3_ref), dils))]
    o1, o2, o3 = [natural(ref, dil, (2 * n + 1) * nsl) for n, (ref, dil) in enumerate(zip((o1_ref, o2_ref, o3_ref), dils))]
    m = jnp.maximum(jnp.maximum(l1, l2), l3)
    e1, e2, e3 = jnp.exp(l1 - m), jnp.exp(l2 - m), jnp.exp(l3 - m)
    ob = (e1 * o1 + e2 * o2 + e3 * o3) / (e1 + e2 + e3)
    y = (x_ref[0]
         + jnp.dot(oa_ref[0], wa_ref[...], preferred_element_type=F32)
         + jnp.dot(ob.astype(BF16), wb_ref[...], preferred_element_type=F32))
    x1_ref[0] = y
    xn_ref[0] = _rms(y, g_ref[...]).astype(BF16)


def _out_proj(x, oa, obs, lses, w_out, g, tm=512):
    b, s, d = x.shape
    wa = w_out[0:A_W].astype(BF16)
    wb = w_out[A_W:].astype(BF16)
    tok = lambda i, j: (i, j, 0)
    const = lambda i, j: (0, 0)
    dil_specs = [pl.BlockSpec((1, tm // dil, dil * B_W), tok) for _, dil in DIL_PATTERNS]
    return pl.pallas_call(
        _out_proj_kernel,
        grid=(b, s // tm),
        in_specs=[pl.BlockSpec((1, tm, d), tok), pl.BlockSpec((1, tm, A_W), tok)] + dil_specs + dil_specs +
                 [pl.BlockSpec((A_W, d), const), pl.BlockSpec((B_W, d), const), pl.BlockSpec((1, d), const)],
        out_specs=[pl.BlockSpec((1, tm, d), tok), pl.BlockSpec((1, tm, d), tok)],
        out_shape=[jax.ShapeDtypeStruct((b, s, d), F32), jax.ShapeDtypeStruct((b, s, d), BF16)],
        scratch_shapes=[pltpu.VMEM((2 * len(DIL_PATTERNS) * B_W // LANES, tm, LANES), F32)],
        compiler_params=_cparams(("parallel", "parallel")),
        name="out_proj",
    )(x, oa, *obs, *lses, wa, wb, g.reshape(1, d))


W_ROWS = N_KEYS // 2
I1_GROUP = 8
PEER_N1 = 16


def _cand_tables():
    groups = [(0, 16), (1, 8)] + [(a, 8) for a in range(2, 8)]
    a_idx, b_idx = [], []
    for a, n in groups:
        a_idx += [a] * n
        b_idx += list(range(n))
    a_idx += list(range(8, 16))
    b_idx += [0] * 8
    a_idx, b_idx = np.array(a_idx), np.array(b_idx)
    ok = (a_idx + 1) * (b_idx + 1) <= PEER_TOPK
    pos = np.where(ok, a_idx * PEER_TOPK + b_idx, -1)
    return groups, pos.astype(np.int32)


def _top_rows(s, k):
    n = s.shape[0]
    iota = lax.broadcasted_iota(jnp.int32, s.shape, 0)
    vals, idxs = [], []
    for _ in range(k):
        m = jnp.max(s, axis=0, keepdims=True)
        ix = jnp.min(jnp.where(s == m, iota, n), axis=0, keepdims=True)
        vals.append(m)
        idxs.append(ix)
        s = jnp.where(iota == ix, NEG_INF, s)
    return jnp.concatenate(vals, axis=0), jnp.concatenate(idxs, axis=0)


def _batcher_pairs(n):
    pairs = []
    p = 1
    while p < n:
        k = p
        while k >= 1:
            for j in range(k % p, n - k, 2 * k):
                for i in range(min(k, n - j - k)):
                    if (i + j) // (2 * p) == (i + j + k) // (2 * p):
                        pairs.append((i + j, i + j + k))
            k //= 2
        p *= 2
    return pairs


def _top_rows_sorted(s, k, rank=None):
    n, t = s.shape
    if t > LANES:
        parts = [_top_rows_sorted(s[:, c:c + LANES], k, None if rank is None else rank[:, c:c + LANES])
                 for c in range(0, t, LANES)]
        return tuple(jnp.concatenate(z, axis=1) for z in zip(*parts))
    sl = 8
    ng = n // sl
    x = [s[g * sl:(g + 1) * sl] for g in range(ng)]
    if rank is None:
        sub = lax.broadcasted_iota(jnp.int32, (sl, t), 0)
        r = [sub + g * sl for g in range(ng)]
    else:
        r = [rank[g * sl:(g + 1) * sl] for g in range(ng)]

    def before(xa, ra, xb, rb):
        return (xa > xb) | ((xa == xb) & (ra < rb))

    for a, b in [ab for ab in _batcher_pairs(1 << (ng - 1).bit_length()) if ab[1] < ng]:
        keep = before(x[a], r[a], x[b], r[b])
        x[a], x[b] = jnp.where(keep, x[a], x[b]), jnp.where(keep, x[b], x[a])
        r[a], r[b] = jnp.where(keep, r[a], r[b]), jnp.where(keep, r[b], r[a])

    vals, idxs = [], []
    for p in range(k):
        wv, wr = x[0], r[0]
        for shift in (4, 2, 1):
            pv, pr = pltpu.roll(wv, shift, 0), pltpu.roll(wr, shift, 0)
            keep = before(wv, wr, pv, pr) | (wr == pr)
            wv, wr = jnp.where(keep, wv, pv), jnp.where(keep, wr, pr)
        vals.append(wv[0:1])
        idxs.append(wr[0:1])
        popped = r[0] == wr
        for d in range(min(ng, k - p - 1)):
            if d + 1 < ng:
                x[d] = jnp.where(popped, x[d + 1], x[d])
                r[d] = jnp.where(popped, r[d + 1], r[d])
            else:
                x[d] = jnp.where(popped, NEG_INF, x[d])
    return jnp.concatenate(vals, axis=0), jnp.concatenate(idxs, axis=0)


def _peer_select_kernel(xn_ref, wq_t_ref, keys_ref, pos_ref, w_ref,
                        q_ref, v_ref, ix_ref, e_ref, gt_ref, i1_ref, i2_ref, g_ref, *, groups):
    nt = (((1,), (1,)), ((), ()))
    tt = xn_ref.shape[0]

    @pl.when(pl.program_id(0) == 0)
    def _no_previous_tile():
        for ref in (i1_ref, i2_ref, g_ref):
            ref[...] = jnp.zeros(ref.shape, F32)

    q_ref[...] = lax.dot_general(wq_t_ref[...], xn_ref[...], nt,
                                 preferred_element_type=F32).astype(BF16)

    row = lax.broadcasted_iota(jnp.int32, (N_KEYS, N_KEYS), 0)
    pr = row & (W_ROWS - 1)
    half = I1_GROUP // 2
    i1_of_row = ((pr // half) * I1_GROUP + (pr % half) + half * (row // W_ROWS)).astype(F32)
    sub = row.astype(F32)
    hi_mask = jnp.uint32(0xFFFF0000)
    nch, rpc = w_ref.shape[0], W_ROWS // w_ref.shape[0]

    def gate_words(t):
        r1 = i1_ref[pl.ds(t, 1), :]
        r2 = i2_ref[pl.ds(t, 1), :]
        gg = g_ref[pl.ds(t, 1), :]
        lt = jnp.where(i1_of_row == r1, 1.0, 0.0).astype(BF16)
        rt = jnp.where(sub == r2, gg, 0.0).astype(BF16)
        w = lax.dot_general(lt, rt, nt, preferred_element_type=F32)
        bits = lax.bitcast_convert_type(w.astype(BF16).astype(F32), jnp.uint32)
        words = (bits[W_ROWS:] & hi_mask) | (bits[0:W_ROWS] >> 16)
        w_ref[:, pl.ds(pl.multiple_of(t * rpc, rpc), rpc), :] = words.reshape(nch, rpc, N_KEYS)

    per_half = tt // (2 * PEER_HEADS)

    def half_topk(hc, carry):
        qh = q_ref[pl.ds(pl.multiple_of(hc * PEER_HALF, PEER_HALF), PEER_HALF), :]
        s = jnp.dot(keys_ref[hc % 2], qh, preferred_element_type=F32)
        for u in range(per_half):
            gate_words(hc * per_half + u)
        vals, idxs = _top_rows_sorted(s, PEER_TOPK)
        v_ref[hc] = vals
        ix_ref[hc] = idxs
        return carry

    lax.fori_loop(0, 2 * PEER_HEADS, half_topk, 0, unroll=2)

    pos_tab = pos_ref[...]

    def head_select(h, carry):
        v1, v2 = v_ref[2 * h], v_ref[2 * h + 1]
        i1, i2 = ix_ref[2 * h], ix_ref[2 * h + 1]
        v1g, v2g, i1g, i2g = [], [], [], []
        for a, n in groups:
            v1g.append(jnp.broadcast_to(v1[a:a + 1], (n, tt)))
            i1g.append(jnp.broadcast_to(i1[a:a + 1], (n, tt)))
            v2g.append(v2[0:n])
            i2g.append(i2[0:n])
        v1g.append(v1[8:16]); i1g.append(i1[8:16])
        v2g.append(jnp.broadcast_to(v2[0:1], (8, tt))); i2g.append(jnp.broadcast_to(i2[0:1], (8, tt)))
        cand = jnp.concatenate(v1g, axis=0) + jnp.concatenate(v2g, axis=0)
        eidx = jnp.concatenate(i1g, axis=0) * N_KEYS + jnp.concatenate(i2g, axis=0)
        cand = jnp.where(pos_tab >= 0, cand, NEG_INF)
        key = pos_tab * (N_KEYS * N_KEYS) + eidx
        top, sel = _top_rows_sorted(cand, PEER_TOPK, rank=key)
        ek = sel & (N_KEYS * N_KEYS - 1)
        ex = jnp.exp(top - top[0:1])
        gate = ex / jnp.sum(ex, axis=0, keepdims=True)
        e_ref[h] = ek.astype(F32)
        gt_ref[h] = gate
        return carry

    lax.fori_loop(0, PEER_HEADS, head_select, 0, unroll=2)

    ef = e_ref[...].reshape(PEER_HEADS * PEER_TOPK, tt)
    i1f = jnp.floor(ef * (1.0 / N_KEYS))
    i1_ref[...] = i1f.T
    i2_ref[...] = (ef - i1f * N_KEYS).T
    g_ref[...] = gt_ref[...].reshape(PEER_HEADS * PEER_TOPK, tt).T


def _peer_select(xn, wq, keys1, keys2, n1, tt=256):
    t, d = xn.shape
    groups, pos = _cand_tables()
    pos_tab = jnp.broadcast_to(jnp.asarray(pos)[:, None], (pos.shape[0], tt))
    nslot = PEER_HEADS * PEER_TOPK
    qw = 2 * PEER_HEADS * PEER_HALF
    nch = N_KEYS // n1
    rpc = W_ROWS // nch
    const = lambda i: (0, 0)
    kern = functools.partial(_peer_select_kernel, groups=groups)
    ntile = t // tt
    return pl.pallas_call(
        kern,
        grid=(ntile + 1,),
        in_specs=[pl.BlockSpec((tt, d), lambda i: (jnp.minimum(i, ntile - 1), 0)),
                  pl.BlockSpec((qw, d), const),
                  pl.BlockSpec((2, N_KEYS, PEER_HALF), lambda i: (0, 0, 0)),
                  pl.BlockSpec((pos.shape[0], tt), const)],
        out_specs=pl.BlockSpec((nch, tt * rpc, N_KEYS), lambda i: (0, jnp.maximum(i - 1, 0), 0)),
        out_shape=jax.ShapeDtypeStruct((nch, t * rpc, N_KEYS), jnp.uint32),
        scratch_shapes=[pltpu.VMEM((qw, tt), BF16),
                        pltpu.VMEM((2 * PEER_HEADS, PEER_TOPK, tt), F32),
                        pltpu.VMEM((2 * PEER_HEADS, PEER_TOPK, tt), jnp.int32),
                        pltpu.VMEM((PEER_HEADS, PEER_TOPK, tt), F32),
                        pltpu.VMEM((PEER_HEADS, PEER_TOPK, tt), F32),
                        pltpu.VMEM((tt, nslot), F32),
                        pltpu.VMEM((tt, nslot), F32),
                        pltpu.VMEM((tt, nslot), F32)],
        compiler_params=_cparams(("arbitrary",)),
        name="peer_select",
    )(xn, wq.T.astype(BF16), jnp.stack([keys1, keys2]).astype(BF16), pos_tab)


def _gelu(x):
    c = math.sqrt(2.0 / math.pi)
    return 0.5 * x * (1.0 + jnp.tanh(c * (x + 0.044715 * (x * x * x))))


def _peer_experts_kernel(x1_ref, xn_ref, w_ref, u_ref, v_ref, fg_ref, y_ref, acc_ref, *, tt, n1):
    c = pl.program_id(1)
    nc = pl.num_programs(1)
    nt = (((1,), (1,)), ((), ()))
    hi_mask = jnp.uint32(0xFFFF0000)
    half = I1_GROUP // 2
    rpc = n1 // 2

    @pl.when(c == 0)
    def _init():
        acc_ref[...] = jnp.zeros(acc_ref.shape, F32)

    a = lax.dot_general(xn_ref[...], u_ref[...], nt, preferred_element_type=F32)
    lo, hi = [], []
    for g in range(n1 // I1_GROUP):
        for u in range(half):
            words = w_ref[0, pl.ds(g * half + u, tt, stride=rpc), :]
            lo.append(lax.bitcast_convert_type(words << 16, F32))
            hi.append(lax.bitcast_convert_type(words & hi_mask, F32))
    wc = jnp.concatenate(
        [part[g * half + u] for g in range(n1 // I1_GROUP) for part in (lo, hi) for u in range(half)], axis=1)
    z = (_gelu(a) * wc).astype(BF16)
    acc_ref[...] += jnp.dot(z, v_ref[...], preferred_element_type=F32)

    @pl.when(c == nc - 1)
    def _fin():
        y_ref[...] = _rms(x1_ref[...] + acc_ref[...], fg_ref[...])


def _peer_experts(x1, xn, words, u_tab, v_tab, final_g, n1, tt=512):
    t, d = x1.shape
    ne = u_tab.shape[0]
    ec = n1 * N_KEYS
    rpc = n1 // 2
    tok = lambda i, c: (i, 0)
    kern = functools.partial(_peer_experts_kernel, tt=tt, n1=n1)
    return pl.pallas_call(
        kern,
        grid=(t // tt, ne // ec),
        in_specs=[pl.BlockSpec((tt, d), tok), pl.BlockSpec((tt, d), tok),
                  pl.BlockSpec((1, tt * rpc, N_KEYS), lambda i, c: (c, i, 0)),
                  pl.BlockSpec((ec, d), lambda i, c: (c, 0)),
                  pl.BlockSpec((ec, d), lambda i, c: (c, 0)),
                  pl.BlockSpec((1, d), lambda i, c: (0, 0))],
        out_specs=pl.BlockSpec((tt, d), tok),
        out_shape=jax.ShapeDtypeStruct((t, d), F32),
        scratch_shapes=[pltpu.VMEM((tt, d), F32)],
        compiler_params=_cparams(("parallel", "arbitrary")),
        name="peer_experts",
    )(x1, xn, words, u_tab, v_tab, final_g.reshape(1, d))


def _trunk(x, norm1_g, w_in, lam_vecs, subln_g, w_out, norm2_g, wq, keys1, keys2, u_bf, v_bf, final_g):
    b, s, d = x.shape
    qa_t, ka, va_t, qkv = _in_proj(x, norm1_g, w_in)
    oa = _diff_attn(qa_t, ka, va_t, lam_vecs, subln_g)
    obs, lses = [], []
    for (window, dil), (q2, k2, v2) in zip(DIL_PATTERNS, qkv):
        o, lse = _dil_attn(q2, k2, v2, window, dil)
        obs.append(o)
        lses.append(lse)
    x1, xn2 = _out_proj(x, oa, obs, lses, w_out, norm2_g)
    x1, xn2 = x1.reshape(b * s, d), xn2.reshape(b * s, d)
    words = _peer_select(xn2, wq, keys1, keys2, PEER_N1)
    y = _peer_experts(x1, xn2, words, u_bf, v_bf, final_g, PEER_N1)
    return y.reshape(b, s, d)


def kernel(x_prompt, x_sample, norm1_g, w_in, lambda_q1, lambda_k1, lambda_q2, lambda_k2, subln_g, w_out,
           norm2_g, peer_wq, peer_keys1, peer_keys2, peer_u, peer_v, final_g):
    lam_vecs = jnp.concatenate([lambda_q1, lambda_k1, lambda_q2, lambda_k2], axis=0).astype(F32)
    u_bf = peer_u[0].astype(BF16)
    v_bf = peer_v[0].astype(BF16)
    args = (norm1_g[0], w_in[0], lam_vecs, subln_g[0], w_out[0], norm2_g[0], peer_wq[0],
            peer_keys1[0], peer_keys2[0], u_bf, v_bf, final_g)
    return (_trunk(x_prompt, *args), _trunk(x_sample, *args))
```

```python
import functools
import math

import jax
import jax.numpy as jnp
import numpy as np
from jax import lax
from jax.experimental import pallas as pl
from jax.experimental.pallas import tpu as pltpu

D_MODEL = 1024
HEAD_DIM = 64
DIFF_HEADS = 4
DIFF_V_DIM = 2 * HEAD_DIM
DIL_HEADS = 8
DIL_PATTERNS = ((128, 1), (512, 4), (2048, 16))
A_W = DIFF_HEADS * 2 * HEAD_DIM
B_W = DIL_HEADS * HEAD_DIM
PEER_HEADS = 8
N_KEYS = 128
PEER_HALF = 128
PEER_TOPK = 16
EPS = 1e-6
LAMBDA_INIT = 0.8 - 0.6 * math.exp(-0.3 * 0)

LOG2E = math.log2(math.e)
ATT_CHUNK = 256
VT_ROWS = DIFF_V_DIM + 16
LANES = 128
VMEM_LIMIT = 56 * 1024 * 1024

F32 = jnp.float32
BF16 = jnp.bfloat16
NEG_INF = float("-inf")


def _cparams(sem):
    return pltpu.CompilerParams(dimension_semantics=sem, vmem_limit_bytes=VMEM_LIMIT)


def _unrolled_loop(lo, hi, body, unroll):
    trips = (hi - lo) // unroll

    def group(g, carry):
        for u in range(unroll):
            body(lo + g * unroll + u, carry)
        return carry

    lax.fori_loop(0, trips, group, 0)
    lax.fori_loop(lo + trips * unroll, hi, body, 0)


def _rms(x, g):
    return x * lax.rsqrt(jnp.mean(x * x, axis=-1, keepdims=True) + EPS) * g


def _in_proj_kernel(x_ref, g_ref, wqa_t_ref, wva_t_ref, wrest_ref, qa_t_ref, ka_ref, va_t_ref, *rest_refs, tkc):
    dil_refs, slab_ref = rest_refs[:-1], rest_refs[-1]
    xn = _rms(x_ref[0], g_ref[...]).astype(BF16)
    tm = xn.shape[0]
    nt = (((1,), (1,)), ((), ()))
    qa_t_ref[0] = lax.dot_general(wqa_t_ref[...], xn, nt, preferred_element_type=F32).astype(BF16)
    va_t = lax.dot_general(wva_t_ref[...], xn, nt, preferred_element_type=F32).astype(BF16)
    ones = jnp.ones((VT_ROWS - DIFF_V_DIM, tkc), BF16)
    for hd in range(DIFF_HEADS):
        for c in range(tm // tkc):
            va_t_ref[0, hd, c, 0:DIFF_V_DIM, :] = va_t[hd * DIFF_V_DIM:(hd + 1) * DIFF_V_DIM, c * tkc:(c + 1) * tkc]
            va_t_ref[0, hd, c, DIFF_V_DIM:VT_ROWS, :] = ones
    rest = jnp.dot(xn, wrest_ref[...], preferred_element_type=F32)
    ka_ref[0] = rest[:, 0:A_W].astype(BF16)
    nsl = B_W // LANES
    for a in range(3):
        cols = rest[:, (1 + a) * B_W:(2 + a) * B_W]
        dil_refs[a][0] = cols.astype(BF16)
        for sl in range(nsl):
            slab_ref[a * nsl + sl] = cols[:, sl * LANES:(sl + 1) * LANES]
    for pi, (_, dil) in enumerate(DIL_PATTERNS[1:]):
        for a in range(3):
            out = dil_refs[3 * (pi + 1) + a]
            for r in range(dil):
                for sl in range(nsl):
                    rows = slab_ref[a * nsl + sl, pl.ds(r, tm // dil, stride=dil), :]
                    out[0, :, r * B_W + sl * LANES:r * B_W + (sl + 1) * LANES] = rows.astype(BF16)


def _in_proj(x, g, w_in, tm=512, tkc=ATT_CHUNK):
    b, s, d = x.shape
    assert DIL_PATTERNS[0][1] == 1
    scale = HEAD_DIM ** -0.5
    wqa_t = (w_in[:, 0:A_W] * (scale * LOG2E)).T.astype(BF16)
    wva_t = w_in[:, 2 * A_W:3 * A_W].T.astype(BF16)
    wrest = jnp.concatenate(
        [w_in[:, A_W:2 * A_W], w_in[:, 3 * A_W:4 * A_W] * scale, w_in[:, 4 * A_W:]], axis=1).astype(BF16)
    tok = lambda i, j: (i, j, 0)
    tr = lambda i, j: (i, 0, j)
    const = lambda i, j: (0, 0)
    dil_specs, dil_shapes = [], []
    for _, dil in DIL_PATTERNS:
        dil_specs += [pl.BlockSpec((1, tm // dil, dil * B_W), tok)] * 3
        dil_shapes += [jax.ShapeDtypeStruct((b, s // dil, dil * B_W), BF16)] * 3
    outs = pl.pallas_call(
        functools.partial(_in_proj_kernel, tkc=tkc),
        grid=(b, s // tm),
        in_specs=[pl.BlockSpec((1, tm, d), tok),
                  pl.BlockSpec((1, d), const),
                  pl.BlockSpec((A_W, d), const),
                  pl.BlockSpec((A_W, d), const),
                  pl.BlockSpec((d, 4 * A_W), const)],
        out_specs=[pl.BlockSpec((1, A_W, tm), tr),
                   pl.BlockSpec((1, tm, A_W), tok),
                   pl.BlockSpec((1, DIFF_HEADS, tm // tkc, VT_ROWS, tkc), lambda i, j: (i, 0, j, 0, 0))] + dil_specs,
        out_shape=[jax.ShapeDtypeStruct((b, A_W, s), BF16),
                   jax.ShapeDtypeStruct((b, s, A_W), BF16),
                   jax.ShapeDtypeStruct((b, DIFF_HEADS, s // tkc, VT_ROWS, tkc), BF16)] + dil_shapes,
        scratch_shapes=[pltpu.VMEM((3 * B_W // LANES, tm, LANES), F32)],
        compiler_params=_cparams(("parallel", "parallel")),
        name="in_proj",
    )(x, g.reshape(1, d), wqa_t, wva_t, wrest)
    qkv = [tuple(outs[3 + 3 * p:6 + 3 * p]) for p in range(len(DIL_PATTERNS))]
    return outs[0], outs[1], outs[2], qkv


def _bf16_split3(x):
    x = np.asarray(x, np.float32)
    pieces = []
    for _ in range(3):
        p = x.astype(jnp.bfloat16).astype(np.float32)
        pieces.append(p.astype(np.float64))
        x = (x - p).astype(np.float32)
    return pieces


def _alibi_tables(tq, tkc):
    slopes = 2.0 ** (-8.0 * np.arange(1, DIFF_HEADS + 1) / DIFF_HEADS)
    a = (slopes * LOG2E).astype(np.float32)
    kx = np.zeros((tkc, LANES), np.float32)
    s_rel = np.arange(tkc, dtype=np.float32)
    kx[:, 0:3] = s_rel[:, None]
    kx[:, 3:6] = 1.0
    qx = np.zeros((DIFF_HEADS, 2, LANES, tq), np.float64)
    t_rel = np.arange(tq, dtype=np.float32)
    for hd in range(DIFF_HEADS):
        a3 = _bf16_split3(a[hd])
        f3 = _bf16_split3(-(a[hd] * t_rel).astype(np.float32))
        for r in range(3):
            qx[hd, 0, r, :] = a3[r]
            qx[hd, 0, 3 + r, :] = f3[r]
    qx[:, 1] = -qx[:, 0]
    rel = s_rel[:, None] - t_rel[None, :]
    ndiag = tq // tkc
    diag = np.stack([np.stack([-(a[hd] * np.abs(rel + d * tkc)) for d in range(ndiag)])
                     for hd in range(DIFF_HEADS)]).astype(np.float32)
    return (jnp.asarray(a), jnp.asarray(kx, F32).astype(BF16), jnp.asarray(qx, F32).astype(BF16),
            jnp.asarray(diag))


def _diff_attn_kernel(a_ref, qt_ref, k_ref, vt_ref, kx_ref, qx_ref, diag_ref, lam_ref, sg_ref, o_ref,
                      qs_ref, m_ref, acc_ref, s_ref, *, tq, tkc, unroll):
    h = pl.program_id(1)
    i = pl.program_id(2)
    nk = k_ref.shape[1] // tkc
    ndiag = tq // tkc
    a_h = a_ref[h]

    qt = qt_ref[0]
    zero = jnp.zeros((HEAD_DIM, tq), BF16)
    for side in range(2):
        qs_ref[side, 0:HEAD_DIM, 0:tq] = qt[0:HEAD_DIM]
        qs_ref[side, 0:HEAD_DIM, tq:2 * tq] = zero
        qs_ref[side, HEAD_DIM:DIFF_V_DIM, 0:tq] = zero
        qs_ref[side, HEAD_DIM:DIFF_V_DIM, tq:2 * tq] = qt[HEAD_DIM:]
        qs_ref[side, DIFF_V_DIM:, 0:tq] = qx_ref[0, side]
        qs_ref[side, DIFF_V_DIM:, tq:2 * tq] = qx_ref[0, side]
    m_ref[...] = jnp.full(m_ref.shape, NEG_INF, F32)
    acc_ref[...] = jnp.zeros(acc_ref.shape, F32)

    def update(j, s, c):
        m_old = m_ref[...]
        m_new = jnp.maximum(m_old, jnp.max(s, axis=0, keepdims=True) + c)
        alpha = jnp.exp2(m_old - m_new)
        p = jnp.exp2(s - (m_new - c)).astype(BF16)
        acc_ref[...] = alpha * acc_ref[...] + jnp.dot(vt_ref[0, 0, j], p, preferred_element_type=F32)
        m_ref[...] = m_new

    n_left = i * ndiag

    def scores_diag(d, slot):
        k0 = pl.multiple_of((n_left + d) * tkc, tkc)
        s_ref[slot] = jnp.dot(k_ref[0, pl.ds(k0, tkc), :], qs_ref[0, 0:DIFF_V_DIM, :],
                              preferred_element_type=F32)

    def consume_diag(d, slot):
        bias = diag_ref[0, d]
        update(n_left + d, s_ref[slot] + jnp.concatenate([bias, bias], axis=1), jnp.float32(0.0))

    cnt = nk - ndiag
    last = cnt - 1

    def chunk_of(n):
        n = jnp.minimum(n, last)
        side = (n >= n_left).astype(jnp.int32)
        j = n + side * ndiag
        dist = (1 - 2 * side) * (i * tq - j * tkc)
        return j, side, -a_h * dist.astype(F32)

    def scores(n, slot):
        j, side, _ = chunk_of(n)
        k0 = pl.multiple_of(j * tkc, tkc)
        lhs = jnp.concatenate([k_ref[0, pl.ds(k0, tkc), :], kx_ref[...]], axis=1)
        s_ref[slot] = jnp.dot(lhs, qs_ref[side], preferred_element_type=F32)

    def consume(n, slot):
        j, _, c = chunk_of(n)
        update(j, s_ref[slot], c)

    npeel = cnt % unroll
    head = [(scores_diag, consume_diag, d) for d in range(ndiag)] + [(scores, consume, n) for n in range(npeel)]
    assert len(head) % 2 == 0 and unroll % 2 == 0
    head[0][0](head[0][2], 0)
    for t, (_, cons, arg) in enumerate(head):
        if t + 1 < len(head):
            head[t + 1][0](head[t + 1][2], (t + 1) % 2)
        elif npeel < cnt:
            scores(npeel, 0)
        cons(arg, t % 2)

    def group(g, carry):
        for u in range(unroll):
            scores(npeel + g * unroll + u + 1, (u + 1) % 2)
            consume(npeel + g * unroll + u, u % 2)
        return carry

    lax.fori_loop(0, (cnt - npeel) // unroll, group, 0)

    lam_v = lam_ref[...]
    lam = (jnp.exp(jnp.sum(lam_v[0:1] * lam_v[1:2], axis=1, keepdims=True))
           - jnp.exp(jnp.sum(lam_v[2:3] * lam_v[3:4], axis=1, keepdims=True)) + LAMBDA_INIT)
    acc = acc_ref[...]
    o = acc[0:DIFF_V_DIM] / acc[DIFF_V_DIM:DIFF_V_DIM + 1]
    o = o[:, 0:tq] - lam * o[:, tq:2 * tq]
    ms = jnp.mean(o * o, axis=0, keepdims=True)
    o = o * lax.rsqrt(ms + EPS) * sg_ref[...] * (1.0 - LAMBDA_INIT)
    o_ref[0] = o.T.astype(BF16)


def _diff_attn(qa_t, ka, va_t, lam_vecs, subln_g, tq=512, tkc=ATT_CHUNK, unroll=10):
    b, _, s = qa_t.shape
    nk = s // tkc
    a, kx, qx, diag = _alibi_tables(tq, tkc)
    kern = functools.partial(_diff_attn_kernel, tq=tq, tkc=tkc, unroll=unroll)
    c2 = lambda b_, h, i: (0, 0)
    return pl.pallas_call(
        kern,
        grid=(b, DIFF_HEADS, s // tq),
        in_specs=[pl.BlockSpec(memory_space=pltpu.SMEM),
                  pl.BlockSpec((1, DIFF_V_DIM, tq), lambda b_, h, i: (b_, h, i)),
                  pl.BlockSpec((1, s, DIFF_V_DIM), lambda b_, h, i: (b_, 0, h)),
                  pl.BlockSpec((1, 1, nk, VT_ROWS, tkc), lambda b_, h, i: (b_, h, 0, 0, 0)),
                  pl.BlockSpec((tkc, LANES), c2),
                  pl.BlockSpec((1, 2, LANES, tq), lambda b_, h, i: (h, 0, 0, 0)),
                  pl.BlockSpec((1, tq // tkc, tkc, tq), lambda b_, h, i: (h, 0, 0, 0)),
                  pl.BlockSpec((4, HEAD_DIM), c2),
                  pl.BlockSpec((DIFF_V_DIM, 1), c2)],
        out_specs=pl.BlockSpec((1, tq, DIFF_V_DIM), lambda b_, h, i: (b_, i, h)),
        out_shape=jax.ShapeDtypeStruct((b, s, A_W), BF16),
        scratch_shapes=[pltpu.VMEM((2, 2 * LANES, 2 * tq), BF16),
                        pltpu.VMEM((1, 2 * tq), F32),
                        pltpu.VMEM((VT_ROWS, 2 * tq), F32),
                        pltpu.VMEM((2, tkc, 2 * tq), F32)],
        compiler_params=_cparams(("parallel", "parallel", "arbitrary")),
        name="diff_attn",
    )(a, qa_t, ka, va_t, kx, qx, diag, lam_vecs, subln_g.reshape(DIFF_V_DIM, 1))


def _dil_attn_kernel(bias_ref, q_ref, kp_ref, kc_ref, kn_ref, vp_ref, vc_ref, vn_ref,
                     o_ref, lse_ref, s_ref, *, n_sub, blk, qt):
    i = pl.program_id(2)
    nsb = qt // blk
    col = lax.broadcasted_iota(jnp.int32, (1, 3 * blk), 1)
    lane = lax.broadcasted_iota(jnp.int32, (blk, LANES), 1)
    first = lane < HEAD_DIM
    nt = (((1,), (1,)), ((), ()))

    def window(sb, prev_ref, cur_ref, next_ref, cols):
        parts = [prev_ref[0, :, cols] if sb == 0 else cur_ref[0, (sb - 1) * blk:sb * blk, cols],
                 cur_ref[0, sb * blk:(sb + 1) * blk, cols],
                 next_ref[0, :, cols] if sb == nsb - 1 else cur_ref[0, (sb + 1) * blk:(sb + 2) * blk, cols]]
        return jnp.concatenate(parts, axis=0)

    def scores(sb, hp, slot):
        cols = slice(hp * LANES, (hp + 1) * LANES)
        q = q_ref[0, sb * blk:(sb + 1) * blk, cols]
        zq = jnp.zeros_like(q)
        q2 = jnp.concatenate([jnp.where(first, q, zq), jnp.where(first, zq, q)], axis=0)
        s_ref[slot] = lax.dot_general(q2, window(sb, kp_ref, kc_ref, kn_ref, cols), nt,
                                      preferred_element_type=F32)

    def finish(sb, hp, slot):
        cols = slice(hp * LANES, (hp + 1) * LANES)
        s = s_ref[slot] + jnp.concatenate([bias_ref[2 * hp], bias_ref[2 * hp + 1]], axis=0)
        if sb == 0 or sb == nsb - 1:
            kidx = i * qt + (sb - 1) * blk + col
            s = jnp.where((kidx >= 0) & (kidx < n_sub), s, NEG_INF)
        m = jnp.max(s, axis=1, keepdims=True)
        p = jnp.exp(s - m)
        l = jnp.sum(p, axis=1, keepdims=True)
        o = jnp.dot(p.astype(BF16), window(sb, vp_ref, vc_ref, vn_ref, cols), preferred_element_type=F32) / l
        lse = jnp.broadcast_to(m + jnp.log(l), (2 * blk, LANES))
        o_ref[0, sb * blk:(sb + 1) * blk, cols] = jnp.where(first, o[0:blk], o[blk:]).astype(BF16)
        lse_ref[0, sb * blk:(sb + 1) * blk, cols] = jnp.where(first, lse[0:blk], lse[blk:])

    units = [(sb, hp) for sb in range(nsb) for hp in range(DIL_HEADS // 2)]
    nslot = s_ref.shape[0]
    for n in range(min(nslot - 1, len(units))):
        scores(*units[n], n % nslot)
    for n, unit in enumerate(units):
        if n + nslot - 1 < len(units):
            scores(*units[n + nslot - 1], (n + nslot - 1) % nslot)
        finish(*unit, n % nslot)


def _dil_attn(q2, k2, v2, window, dil, blk=128, qt=512):
    shape2 = q2.shape
    b, n_sub, w = shape2[0], shape2[1], shape2[2] // dil
    qt = min(qt, n_sub)
    nq = n_sub // qt
    nblk = n_sub // blk
    half = window // (2 * dil)
    slopes = 2.0 ** (-8.0 * np.arange(1, DIL_HEADS + 1) / DIL_HEADS)
    joff = np.arange(3 * blk)[None, :] - blk - np.arange(blk)[:, None]
    band = np.abs(joff) <= half
    bias = np.where(band[None], -slopes[:, None, None] * (np.abs(joff) * dil)[None], -np.inf).astype(np.float32)
    cur = lambda b_, r, i: (b_, i, r)
    prv = lambda b_, r, i: (b_, jnp.maximum(i * (qt // blk) - 1, 0), r)
    nxt = lambda b_, r, i: (b_, jnp.minimum((i + 1) * (qt // blk), nblk - 1), r)
    big = lambda f: pl.BlockSpec((1, qt, w), f)
    halo = lambda f: pl.BlockSpec((1, blk, w), f)
    kern = functools.partial(_dil_attn_kernel, n_sub=n_sub, blk=blk, qt=qt)
    o, lse = pl.pallas_call(
        kern,
        grid=(b, dil, nq),
        in_specs=[pl.BlockSpec((DIL_HEADS, blk, 3 * blk), lambda b_, r, i: (0, 0, 0)),
                  big(cur), halo(prv), big(cur), halo(nxt), halo(prv), big(cur), halo(nxt)],
        out_specs=[big(cur), big(cur)],
        out_shape=[jax.ShapeDtypeStruct(shape2, BF16), jax.ShapeDtypeStruct(shape2, F32)],
        scratch_shapes=[pltpu.VMEM((3, 2 * blk, 3 * blk), F32)],
        compiler_params=_cparams(("parallel", "parallel", "parallel")),
        name=f"dil_attn_d{dil}",
    )(jnp.asarray(bias), q2, k2, k2, k2, v2, v2, v2)
    return o, lse


def _out_proj_kernel(x_ref, oa_ref, o1_ref, o2_ref, o3_ref, l1_ref, l2_ref, l3_ref,
                     wa_ref, wb_ref, g_ref, x1_ref, xn_ref, slab_ref):
    tm = x_ref.shape[1]
    nsl = B_W // LANES

    def natural(ref, dil, base):
        if dil == 1:
            return ref[0].astype(F32)
        for r in range(dil):
            for sl in range(nsl):
                slab_ref[base + sl, pl.ds(r, tm // dil, stride=dil), :] = (
                    ref[0, :, r * B_W + sl * LANES:r * B_W + (sl + 1) * LANES].astype(F32))
        return jnp.concatenate([slab_ref[base + sl] for sl in range(nsl)], axis=1)

    dils = [dil for _, dil in DIL_PATTERNS]
    l1, l2, l3 = [natural(ref, dil, (2 * n) * nsl) for n, (ref, dil) in enumerate(zip((l1_ref, l2_ref, l3_ref), dils))]
    o1, o2, o3 = [natural(ref, dil, (2 * n + 1) * nsl) for n, (ref, dil) in enumerate(zip((o1_ref, o2_ref, o3_ref), dils))]
    m = jnp.maximum(jnp.maximum(l1, l2), l3)
    e1, e2, e3 = jnp.exp(l1 - m), jnp.exp(l2 - m), jnp.exp(l3 - m)
    ob = (e1 * o1 + e2 * o2 + e3 * o3) / (e1 + e2 + e3)
    y = (x_ref[0]
         + jnp.dot(oa_ref[0], wa_ref[...], preferred_element_type=F32)
         + jnp.dot(ob.astype(BF16), wb_ref[...], preferred_element_type=F32))
    x1_ref[0] = y
    xn_ref[0] = _rms(y, g_ref[...]).astype(BF16)


def _out_proj(x, oa, obs, lses, w_out, g, tm=512):
    b, s, d = x.shape
    wa = w_out[0:A_W].astype(BF16)
    wb = w_out[A_W:].astype(BF16)
    tok = lambda i, j: (i, j, 0)
    const = lambda i, j: (0, 0)
    dil_specs = [pl.BlockSpec((1, tm // dil, dil * B_W), tok) for _, dil in DIL_PATTERNS]
    return pl.pallas_call(
        _out_proj_kernel,
        grid=(b, s // tm),
        in_specs=[pl.BlockSpec((1, tm, d), tok), pl.BlockSpec((1, tm, A_W), tok)] + dil_specs + dil_specs +
                 [pl.BlockSpec((A_W, d), const), pl.BlockSpec((B_W, d), const), pl.BlockSpec((1, d), const)],
        out_specs=[pl.BlockSpec((1, tm, d), tok), pl.BlockSpec((1, tm, d), tok)],
        out_shape=[jax.ShapeDtypeStruct((b, s, d), F32), jax.ShapeDtypeStruct((b, s, d), BF16)],
        scratch_shapes=[pltpu.VMEM((2 * len(DIL_PATTERNS) * B_W // LANES, tm, LANES), F32)],
        compiler_params=_cparams(("parallel", "parallel")),
        name="out_proj",
    )(x, oa, *obs, *lses, wa, wb, g.reshape(1, d))


W_ROWS = N_KEYS // 2
I1_GROUP = 8
PEER_N1 = 16


def _cand_tables():
    groups = [(0, 16), (1, 8)] + [(a, 8) for a in range(2, 8)]
    a_idx, b_idx = [], []
    for a, n in groups:
        a_idx += [a] * n
        b_idx += list(range(n))
    a_idx += list(range(8, 16))
    b_idx += [0] * 8
    a_idx, b_idx = np.array(a_idx), np.array(b_idx)
    ok = (a_idx + 1) * (b_idx + 1) <= PEER_TOPK
    pos = np.where(ok, a_idx * PEER_TOPK + b_idx, -1)
    return groups, pos.astype(np.int32)


def _top_rows(s, k):
    n = s.shape[0]
    iota = lax.broadcasted_iota(jnp.int32, s.shape, 0)
    vals, idxs = [], []
    for _ in range(k):
        m = jnp.max(s, axis=0, keepdims=True)
        ix = jnp.min(jnp.where(s == m, iota, n), axis=0, keepdims=True)
        vals.append(m)
        idxs.append(ix)
        s = jnp.where(iota == ix, NEG_INF, s)
    return jnp.concatenate(vals, axis=0), jnp.concatenate(idxs, axis=0)


def _batcher_pairs(n):
    pairs = []
    p = 1
    while p < n:
        k = p
        while k >= 1:
            for j in range(k % p, n - k, 2 * k):
                for i in range(min(k, n - j - k)):
                    if (i + j) // (2 * p) == (i + j + k) // (2 * p):
                        pairs.append((i + j, i + j + k))
            k //= 2
        p *= 2
    return pairs


def _top_rows_sorted(s, k, rank=None):
    n, t = s.shape
    if t > LANES:
        parts = [_top_rows_sorted(s[:, c:c + LANES], k, None if rank is None else rank[:, c:c + LANES])
                 for c in range(0, t, LANES)]
        return tuple(jnp.concatenate(z, axis=1) for z in zip(*parts))
    sl = 8
    ng = n // sl
    x = [s[g * sl:(g + 1) * sl] for g in range(ng)]
    if rank is None:
        sub = lax.broadcasted_iota(jnp.int32, (sl, t), 0)
        r = [sub + g * sl for g in range(ng)]
    else:
        r = [rank[g * sl:(g + 1) * sl] for g in range(ng)]

    def before(xa, ra, xb, rb):
        return (xa > xb) | ((xa == xb) & (ra < rb))

    for a, b in [ab for ab in _batcher_pairs(1 << (ng - 1).bit_length()) if ab[1] < ng]:
        keep = before(x[a], r[a], x[b], r[b])
        x[a], x[b] = jnp.where(keep, x[a], x[b]), jnp.where(keep, x[b], x[a])
        r[a], r[b] = jnp.where(keep, r[a], r[b]), jnp.where(keep, r[b], r[a])

    vals, idxs = [], []
    for p in range(k):
        wv, wr = x[0], r[0]
        for shift in (4, 2, 1):
            pv, pr = pltpu.roll(wv, shift, 0), pltpu.roll(wr, shift, 0)
            keep = before(wv, wr, pv, pr) | (wr == pr)
            wv, wr = jnp.where(keep, wv, pv), jnp.where(keep, wr, pr)
        vals.append(wv[0:1])
        idxs.append(wr[0:1])
        popped = r[0] == wr
        for d in range(min(ng, k - p - 1)):
            if d + 1 < ng:
                x[d] = jnp.where(popped, x[d + 1], x[d])
                r[d] = jnp.where(popped, r[d + 1], r[d])
            else:
                x[d] = jnp.where(popped, NEG_INF, x[d])
    return jnp.concatenate(vals, axis=0), jnp.concatenate(idxs, axis=0)


def _peer_select_kernel(xn_ref, wq_t_ref, keys_ref, pos_ref, w_ref,
                        q_ref, v_ref, ix_ref, e_ref, gt_ref, i1_ref, i2_ref, g_ref, *, groups):
    nt = (((1,), (1,)), ((), ()))
    tt = xn_ref.shape[0]

    @pl.when(pl.program_id(0) == 0)
    def _no_previous_tile():
        for ref in (i1_ref, i2_ref, g_ref):
            ref[...] = jnp.zeros(ref.shape, F32)

    q_ref[...] = lax.dot_general(wq_t_ref[...], xn_ref[...], nt,
                                 preferred_element_type=F32).astype(BF16)

    row = lax.broadcasted_iota(jnp.int32, (N_KEYS, N_KEYS), 0)
    pr = row & (W_ROWS - 1)
    half = I1_GROUP // 2
    i1_of_row = ((pr // half) * I1_GROUP + (pr % half) + half * (row // W_ROWS)).astype(F32)
    sub = row.astype(F32)
    hi_mask = jnp.uint32(0xFFFF0000)
    nch, rpc = w_ref.shape[0], W_ROWS // w_ref.shape[0]

    def gate_words(t):
        r1 = i1_ref[pl.ds(t, 1), :]
        r2 = i2_ref[pl.ds(t, 1), :]
        gg = g_ref[pl.ds(t, 1), :]
        lt = jnp.where(i1_of_row == r1, 1.0, 0.0).astype(BF16)
        rt = jnp.where(sub == r2, gg, 0.0).astype(BF16)
        w = lax.dot_general(lt, rt, nt, preferred_element_type=F32)
        bits = lax.bitcast_convert_type(w.astype(BF16).astype(F32), jnp.uint32)
        words = (bits[W_ROWS:] & hi_mask) | (bits[0:W_ROWS] >> 16)
        w_ref[:, pl.ds(pl.multiple_of(t * rpc, rpc), rpc), :] = words.reshape(nch, rpc, N_KEYS)

    per_half = tt // (2 * PEER_HEADS)

    def half_topk(hc, carry):
        qh = q_ref[pl.ds(pl.multiple_of(hc * PEER_HALF, PEER_HALF), PEER_HALF), :]
        s = jnp.dot(keys_ref[hc % 2], qh, preferred_element_type=F32)
        for u in range(per_half):
            gate_words(hc * per_half + u)
        vals, idxs = _top_rows_sorted(s, PEER_TOPK)
        v_ref[hc] = vals
        ix_ref[hc] = idxs
        return carry

    lax.fori_loop(0, 2 * PEER_HEADS, half_topk, 0, unroll=2)

    pos_tab = pos_ref[...]

    def head_select(h, carry):
        v1, v2 = v_ref[2 * h], v_ref[2 * h + 1]
        i1, i2 = ix_ref[2 * h], ix_ref[2 * h + 1]
        v1g, v2g, i1g, i2g = [], [], [], []
        for a, n in groups:
            v1g.append(jnp.broadcast_to(v1[a:a + 1], (n, tt)))
            i1g.append(jnp.broadcast_to(i1[a:a + 1], (n, tt)))
            v2g.append(v2[0:n])
            i2g.append(i2[0:n])
        v1g.append(v1[8:16]); i1g.append(i1[8:16])
        v2g.append(jnp.broadcast_to(v2[0:1], (8, tt))); i2g.append(jnp.broadcast_to(i2[0:1], (8, tt)))
        cand = jnp.concatenate(v1g, axis=0) + jnp.concatenate(v2g, axis=0)
        eidx = jnp.concatenate(i1g, axis=0) * N_KEYS + jnp.concatenate(i2g, axis=0)
        cand = jnp.where(pos_tab >= 0, cand, NEG_INF)
        key = pos_tab * (N_KEYS * N_KEYS) + eidx
        top, sel = _top_rows_sorted(cand, PEER_TOPK, rank=key)
        ek = sel & (N_KEYS * N_KEYS - 1)
        ex = jnp.exp(top - top[0:1])
        gate = ex / jnp.sum(ex, axis=0, keepdims=True)
        e_ref[h] = ek.astype(F32)
        gt_ref[h] = gate
        return carry

    lax.fori_loop(0, PEER_HEADS, head_select, 0, unroll=2)

    ef = e_ref[...].reshape(PEER_HEADS * PEER_TOPK, tt)
    i1f = jnp.floor(ef * (1.0 / N_KEYS))
    i1_ref[...] = i1f.T
    i2_ref[...] = (ef - i1f * N_KEYS).T
    g_ref[...] = gt_ref[...].reshape(PEER_HEADS * PEER_TOPK, tt).T


def _peer_select(xn, wq, keys1, keys2, n1, tt=256):
    t, d = xn.shape
    groups, pos = _cand_tables()
    pos_tab = jnp.broadcast_to(jnp.asarray(pos)[:, None], (pos.shape[0], tt))
    nslot = PEER_HEADS * PEER_TOPK
    qw = 2 * PEER_HEADS * PEER_HALF
    nch = N_KEYS // n1
    rpc = W_ROWS // nch
    const = lambda i: (0, 0)
    kern = functools.partial(_peer_select_kernel, groups=groups)
    ntile = t // tt
    return pl.pallas_call(
        kern,
        grid=(ntile + 1,),
        in_specs=[pl.BlockSpec((tt, d), lambda i: (jnp.minimum(i, ntile - 1), 0)),
                  pl.BlockSpec((qw, d), const),
                  pl.BlockSpec((2, N_KEYS, PEER_HALF), lambda i: (0, 0, 0)),
                  pl.BlockSpec((pos.shape[0], tt), const)],
        out_specs=pl.BlockSpec((nch, tt * rpc, N_KEYS), lambda i: (0, jnp.maximum(i - 1, 0), 0)),
        out_shape=jax.ShapeDtypeStruct((nch, t * rpc, N_KEYS), jnp.uint32),
        scratch_shapes=[pltpu.VMEM((qw, tt), BF16),
                        pltpu.VMEM((2 * PEER_HEADS, PEER_TOPK, tt), F32),
                        pltpu.VMEM((2 * PEER_HEADS, PEER_TOPK, tt), jnp.int32),
                        pltpu.VMEM((PEER_HEADS, PEER_TOPK, tt), F32),
                        pltpu.VMEM((PEER_HEADS, PEER_TOPK, tt), F32),
                        pltpu.VMEM((tt, nslot), F32),
                        pltpu.VMEM((tt, nslot), F32),
                        pltpu.VMEM((tt, nslot), F32)],
        compiler_params=_cparams(("arbitrary",)),
        name="peer_select",
    )(xn, wq.T.astype(BF16), jnp.stack([keys1, keys2]).astype(BF16), pos_tab)


def _gelu(x):
    c = math.sqrt(2.0 / math.pi)
    return 0.5 * x * (1.0 + jnp.tanh(c * (x + 0.044715 * (x * x * x))))


def _peer_experts_kernel(x1_ref, xn_ref, w_ref, u_ref, v_ref, fg_ref, y_ref, acc_ref, *, tt, n1):
    c = pl.program_id(1)
    nc = pl.num_programs(1)
    nt = (((1,), (1,)), ((), ()))
    hi_mask = jnp.uint32(0xFFFF0000)
    half = I1_GROUP // 2
    rpc = n1 // 2

    @pl.when(c == 0)
    def _init():
        acc_ref[...] = jnp.zeros(acc_ref.shape, F32)

    a = lax.dot_general(xn_ref[...], u_ref[...], nt, preferred_element_type=F32)
    lo, hi = [], []
    for g in range(n1 // I1_GROUP):
        for u in range(half):
            words = w_ref[0, pl.ds(g * half + u, tt, stride=rpc), :]
            lo.append(lax.bitcast_convert_type(words << 16, F32))
            hi.append(lax.bitcast_convert_type(words & hi_mask, F32))
    wc = jnp.concatenate(
        [part[g * half + u] for g in range(n1 // I1_GROUP) for part in (lo, hi) for u in range(half)], axis=1)
    z = (_gelu(a) * wc).astype(BF16)
    acc_ref[...] += jnp.dot(z, v_ref[...], preferred_element_type=F32)

    @pl.when(c == nc - 1)
    def _fin():
        y_ref[...] = _rms(x1_ref[...] + acc_ref[...], fg_ref[...])


def _peer_experts(x1, xn, words, u_tab, v_tab, final_g, n1, tt=512):
    t, d = x1.shape
    ne = u_tab.shape[0]
    ec = n1 * N_KEYS
    rpc = n1 // 2
    tok = lambda i, c: (i, 0)
    kern = functools.partial(_peer_experts_kernel, tt=tt, n1=n1)
    return pl.pallas_call(
        kern,
        grid=(t // tt, ne // ec),
        in_specs=[pl.BlockSpec((tt, d), tok), pl.BlockSpec((tt, d), tok),
                  pl.BlockSpec((1, tt * rpc, N_KEYS), lambda i, c: (c, i, 0)),
                  pl.BlockSpec((ec, d), lambda i, c: (c, 0)),
                  pl.BlockSpec((ec, d), lambda i, c: (c, 0)),
                  pl.BlockSpec((1, d), lambda i, c: (0, 0))],
        out_specs=pl.BlockSpec((tt, d), tok),
        out_shape=jax.ShapeDtypeStruct((t, d), F32),
        scratch_shapes=[pltpu.VMEM((tt, d), F32)],
        compiler_params=_cparams(("parallel", "arbitrary")),
        name="peer_experts",
    )(x1, xn, words, u_tab, v_tab, final_g.reshape(1, d))


def _trunk(x, norm1_g, w_in, lam_vecs, subln_g, w_out, norm2_g, wq, keys1, keys2, u_bf, v_bf, final_g):
    b, s, d = x.shape
    qa_t, ka, va_t, qkv = _in_proj(x, norm1_g, w_in)
    oa = _diff_attn(qa_t, ka, va_t, lam_vecs, subln_g)
    obs, lses = [], []
    for (window, dil), (q2, k2, v2) in zip(DIL_PATTERNS, qkv):
        o, lse = _dil_attn(q2, k2, v2, window, dil)
        obs.append(o)
        lses.append(lse)
    x1, xn2 = _out_proj(x, oa, obs, lses, w_out, norm2_g)
    x1, xn2 = x1.reshape(b * s, d), xn2.reshape(b * s, d)
    words = _peer_select(xn2, wq, keys1, keys2, PEER_N1)
    y = _peer_experts(x1, xn2, words, u_bf, v_bf, final_g, PEER_N1)
    return y.reshape(b, s, d)


def kernel(x_prompt, x_sample, norm1_g, w_in, lambda_q1, lambda_k1, lambda_q2, lambda_k2, subln_g, w_out,
           norm2_g, peer_wq, peer_keys1, peer_keys2, peer_u, peer_v, final_g):
    lam_vecs = jnp.concatenate([lambda_q1, lambda_k1, lambda_q2, lambda_k2], axis=0).astype(F32)
    u_bf = peer_u[0].astype(BF16)
    v_bf = peer_v[0].astype(BF16)
    args = (norm1_g[0], w_in[0], lam_vecs, subln_g[0], w_out[0], norm2_g[0], peer_wq[0],
            peer_keys1[0], peer_keys2[0], u_bf, v_bf, final_g)
    return (_trunk(x_prompt, *args), _trunk(x_sample, *args))
```

```python
import functools
import math

import jax
import jax.numpy as jnp
import numpy as np
from jax import lax
from jax.experimental import pallas as pl
from jax.experimental.pallas import tpu as pltpu

D_MODEL = 1024
HEAD_DIM = 64
DIFF_HEADS = 4
DIFF_V_DIM = 2 * HEAD_DIM
DIL_HEADS = 8
DIL_PATTERNS = ((128, 1), (512, 4), (2048, 16))
A_W = DIFF_HEADS * 2 * HEAD_DIM
B_W = DIL_HEADS * HEAD_DIM
PEER_HEADS = 8
N_KEYS = 128
PEER_HALF = 128
PEER_TOPK = 16
EPS = 1e-6
LAMBDA_INIT = 0.8 - 0.6 * math.exp(-0.3 * 0)

LOG2E = math.log2(math.e)
ATT_CHUNK = 256
VT_ROWS = DIFF_V_DIM + 16
LANES = 128
VMEM_LIMIT = 56 * 1024 * 1024

F32 = jnp.float32
BF16 = jnp.bfloat16
NEG_INF = float("-inf")


def _cparams(sem):
    return pltpu.CompilerParams(dimension_semantics=sem, vmem_limit_bytes=VMEM_LIMIT)


def _rms(x, g):
    return x * lax.rsqrt(jnp.mean(x * x, axis=-1, keepdims=True) + EPS) * g


def _in_proj_kernel(x_ref, g_ref, wqa_t_ref, wva_t_ref, wrest_ref, qa_t_ref, ka_ref, va_t_ref, *rest_refs, tkc):
    dil_refs, slab_ref = rest_refs[:-1], rest_refs[-1]
    xn = _rms(x_ref[0], g_ref[...]).astype(BF16)
    tm = xn.shape[0]
    nt = (((1,), (1,)), ((), ()))
    qa_t_ref[0] = lax.dot_general(wqa_t_ref[...], xn, nt, preferred_element_type=F32).astype(BF16)
    va_t = lax.dot_general(wva_t_ref[...], xn, nt, preferred_element_type=F32).astype(BF16)
    ones = jnp.ones((VT_ROWS - DIFF_V_DIM, tkc), BF16)
    for hd in range(DIFF_HEADS):
        for c in range(tm // tkc):
            va_t_ref[0, hd, c, 0:DIFF_V_DIM, :] = va_t[hd * DIFF_V_DIM:(hd + 1) * DIFF_V_DIM, c * tkc:(c + 1) * tkc]
            va_t_ref[0, hd, c, DIFF_V_DIM:VT_ROWS, :] = ones
    rest = jnp.dot(xn, wrest_ref[...], preferred_element_type=F32)
    ka_ref[0] = rest[:, 0:A_W].astype(BF16)
    nsl = B_W // LANES
    for a in range(3):
        cols = rest[:, (1 + a) * B_W:(2 + a) * B_W]
        dil_refs[a][0] = cols.astype(BF16)
        for sl in range(nsl):
            slab_ref[a * nsl + sl] = cols[:, sl * LANES:(sl + 1) * LANES]
    for pi, (_, dil) in enumerate(DIL_PATTERNS[1:]):
        for a in range(3):
            out = dil_refs[3 * (pi + 1) + a]
            for r in range(dil):
                for sl in range(nsl):
                    rows = slab_ref[a * nsl + sl, pl.ds(r, tm // dil, stride=dil), :]
                    out[0, :, r * B_W + sl * LANES:r * B_W + (sl + 1) * LANES] = rows.astype(BF16)


def _in_proj(x, g, w_in, tm=512, tkc=ATT_CHUNK):
    b, s, d = x.shape
    assert DIL_PATTERNS[0][1] == 1
    scale = HEAD_DIM ** -0.5
    wqa_t = (w_in[:, 0:A_W] * (scale * LOG2E)).T.astype(BF16)
    wva_t = w_in[:, 2 * A_W:3 * A_W].T.astype(BF16)
    wrest = jnp.concatenate(
        [w_in[:, A_W:2 * A_W], w_in[:, 3 * A_W:4 * A_W] * scale, w_in[:, 4 * A_W:]], axis=1).astype(BF16)
    tok = lambda i, j: (i, j, 0)
    tr = lambda i, j: (i, 0, j)
    const = lambda i, j: (0, 0)
    dil_specs, dil_shapes = [], []
    for _, dil in DIL_PATTERNS:
        dil_specs += [pl.BlockSpec((1, tm // dil, dil * B_W), tok)] * 3
        dil_shapes += [jax.ShapeDtypeStruct((b, s // dil, dil * B_W), BF16)] * 3
    outs = pl.pallas_call(
        functools.partial(_in_proj_kernel, tkc=tkc),
        grid=(b, s // tm),
        in_specs=[pl.BlockSpec((1, tm, d), tok),
                  pl.BlockSpec((1, d), const),
                  pl.BlockSpec((A_W, d), const),
                  pl.BlockSpec((A_W, d), const),
                  pl.BlockSpec((d, 4 * A_W), const)],
        out_specs=[pl.BlockSpec((1, A_W, tm), tr),
                   pl.BlockSpec((1, tm, A_W), tok),
                   pl.BlockSpec((1, DIFF_HEADS, tm // tkc, VT_ROWS, tkc), lambda i, j: (i, 0, j, 0, 0))] + dil_specs,
        out_shape=[jax.ShapeDtypeStruct((b, A_W, s), BF16),
                   jax.ShapeDtypeStruct((b, s, A_W), BF16),
                   jax.ShapeDtypeStruct((b, DIFF_HEADS, s // tkc, VT_ROWS, tkc), BF16)] + dil_shapes,
        scratch_shapes=[pltpu.VMEM((3 * B_W // LANES, tm, LANES), F32)],
        compiler_params=_cparams(("parallel", "parallel")),
        name="in_proj",
    )(x, g.reshape(1, d), wqa_t, wva_t, wrest)
    qkv = [tuple(outs[3 + 3 * p:6 + 3 * p]) for p in range(len(DIL_PATTERNS))]
    return outs[0], outs[1], outs[2], qkv


def _bf16_split3(x):
    x = np.asarray(x, np.float32)
    pieces = []
    for _ in range(3):
        p = x.astype(jnp.bfloat16).astype(np.float32)
        pieces.append(p.astype(np.float64))
        x = (x - p).astype(np.float32)
    return pieces


def _alibi_tables(tq, tkc):
    slopes = 2.0 ** (-8.0 * np.arange(1, DIFF_HEADS + 1) / DIFF_HEADS)
    a = (slopes * LOG2E).astype(np.float32)
    kx = np.zeros((tkc, LANES), np.float32)
    s_rel = np.arange(tkc, dtype=np.float32)
    s_lo = s_rel % 256
    kx[:, 0:3] = s_lo[:, None]
    kx[:, 3:6] = (s_rel - s_lo)[:, None]
    kx[:, 6:9] = 1.0
    qx = np.zeros((DIFF_HEADS, 2, LANES, tq), np.float64)
    t_rel = np.arange(tq, dtype=np.float32)
    for hd in range(DIFF_HEADS):
        a3 = _bf16_split3(a[hd])
        f3 = _bf16_split3(-(a[hd] * t_rel).astype(np.float32))
        for r in range(3):
            qx[hd, 0, r, :] = a3[r]
            qx[hd, 0, 3 + r, :] = a3[r]
            qx[hd, 0, 6 + r, :] = f3[r]
    qx[:, 1] = -qx[:, 0]
    rel = s_rel[:, None] - t_rel[None, :]
    ndiag = tq // tkc
    diag = np.stack([np.stack([-(a[hd] * np.abs(rel + d * tkc)) for d in range(ndiag)])
                     for hd in range(DIFF_HEADS)]).astype(np.float32)
    return (jnp.asarray(a), jnp.asarray(kx, F32).astype(BF16), jnp.asarray(qx, F32).astype(BF16),
            jnp.asarray(diag))


def _diff_attn_kernel(a_ref, qt_ref, k_ref, vt_ref, kx_ref, qx_ref, diag_ref, lam_ref, sg_ref, o_ref,
                      qs_ref, m_ref, acc_ref, s_ref, *, tq, tkc, unroll):
    h = pl.program_id(1)
    i = pl.program_id(2)
    nk = k_ref.shape[1] // tkc
    ndiag = tq // tkc
    a_h = a_ref[h]

    qt = qt_ref[0]
    zero = jnp.zeros((HEAD_DIM, tq), BF16)
    for side in range(2):
        qs_ref[side, 0:HEAD_DIM, 0:tq] = qt[0:HEAD_DIM]
        qs_ref[side, 0:HEAD_DIM, tq:2 * tq] = zero
        qs_ref[side, HEAD_DIM:DIFF_V_DIM, 0:tq] = zero
        qs_ref[side, HEAD_DIM:DIFF_V_DIM, tq:2 * tq] = qt[HEAD_DIM:]
        qs_ref[side, DIFF_V_DIM:, 0:tq] = qx_ref[0, side]
        qs_ref[side, DIFF_V_DIM:, tq:2 * tq] = qx_ref[0, side]
    m_ref[...] = jnp.full(m_ref.shape, NEG_INF, F32)
    acc_ref[...] = jnp.zeros(acc_ref.shape, F32)

    def update(j, s, c):
        m_old = m_ref[...]
        m_new = jnp.maximum(m_old, jnp.max(s, axis=0, keepdims=True) + c)
        alpha = jnp.exp2(m_old - m_new)
        p = jnp.exp2(s - (m_new - c)).astype(BF16)
        acc_ref[...] = alpha * acc_ref[...] + jnp.dot(vt_ref[0, 0, j], p, preferred_element_type=F32)
        m_ref[...] = m_new

    n_left = i * ndiag

    def scores_diag(d, slot):
        k0 = pl.multiple_of((n_left + d) * tkc, tkc)
        s_ref[slot] = jnp.dot(k_ref[0, pl.ds(k0, tkc), :], qs_ref[0, 0:DIFF_V_DIM, :],
                              preferred_element_type=F32)

    def consume_diag(d, slot):
        bias = diag_ref[0, d]
        update(n_left + d, s_ref[slot] + jnp.concatenate([bias, bias], axis=1), jnp.float32(0.0))

    cnt = nk - ndiag
    last = cnt - 1

    def chunk_of(n):
        n = jnp.minimum(n, last)
        side = (n >= n_left).astype(jnp.int32)
        j = n + side * ndiag
        dist = (1 - 2 * side) * (i * tq - j * tkc)
        return j, side, -a_h * dist.astype(F32)

    def scores(n, slot):
        j, side, _ = chunk_of(n)
        k0 = pl.multiple_of(j * tkc, tkc)
        lhs = jnp.concatenate([k_ref[0, pl.ds(k0, tkc), :], kx_ref[...]], axis=1)
        s_ref[slot] = jnp.dot(lhs, qs_ref[side], preferred_element_type=F32)

    def consume(n, slot):
        j, _, c = chunk_of(n)
        update(j, s_ref[slot], c)

    npeel = cnt % unroll
    head = [(scores_diag, consume_diag, d) for d in range(ndiag)] + [(scores, consume, n) for n in range(npeel)]
    assert len(head) % 2 == 0 and unroll % 2 == 0
    head[0][0](head[0][2], 0)
    for t, (_, cons, arg) in enumerate(head):
        if t + 1 < len(head):
            head[t + 1][0](head[t + 1][2], (t + 1) % 2)
        elif npeel < cnt:
            scores(npeel, 0)
        cons(arg, t % 2)

    def group(g, carry):
        for u in range(unroll):
            scores(npeel + g * unroll + u + 1, (u + 1) % 2)
            consume(npeel + g * unroll + u, u % 2)
        return carry

    lax.fori_loop(0, (cnt - npeel) // unroll, group, 0)

    lam_v = lam_ref[...]
    lam = (jnp.exp(jnp.sum(lam_v[0:1] * lam_v[1:2], axis=1, keepdims=True))
           - jnp.exp(jnp.sum(lam_v[2:3] * lam_v[3:4], axis=1, keepdims=True)) + LAMBDA_INIT)
    acc = acc_ref[...]
    o = acc[0:DIFF_V_DIM] / acc[DIFF_V_DIM:DIFF_V_DIM + 1]
    o = o[:, 0:tq] - lam * o[:, tq:2 * tq]
    ms = jnp.mean(o * o, axis=0, keepdims=True)
    o = o * lax.rsqrt(ms + EPS) * sg_ref[...] * (1.0 - LAMBDA_INIT)
    o_ref[0] = o.T.astype(BF16)


def _diff_attn(qa_t, ka, va_t, lam_vecs, subln_g, tq=512, tkc=ATT_CHUNK, unroll=10):
    b, _, s = qa_t.shape
    nk = s // tkc
    a, kx, qx, diag = _alibi_tables(tq, tkc)
    kern = functools.partial(_diff_attn_kernel, tq=tq, tkc=tkc, unroll=unroll)
    c2 = lambda b_, h, i: (0, 0)
    return pl.pallas_call(
        kern,
        grid=(b, DIFF_HEADS, s // tq),
        in_specs=[pl.BlockSpec(memory_space=pltpu.SMEM),
                  pl.BlockSpec((1, DIFF_V_DIM, tq), lambda b_, h, i: (b_, h, i)),
                  pl.BlockSpec((1, s, DIFF_V_DIM), lambda b_, h, i: (b_, 0, h)),
                  pl.BlockSpec((1, 1, nk, VT_ROWS, tkc), lambda b_, h, i: (b_, h, 0, 0, 0)),
                  pl.BlockSpec((tkc, LANES), c2),
                  pl.BlockSpec((1, 2, LANES, tq), lambda b_, h, i: (h, 0, 0, 0)),
                  pl.BlockSpec((1, tq // tkc, tkc, tq), lambda b_, h, i: (h, 0, 0, 0)),
                  pl.BlockSpec((4, HEAD_DIM), c2),
                  pl.BlockSpec((DIFF_V_DIM, 1), c2)],
        out_specs=pl.BlockSpec((1, tq, DIFF_V_DIM), lambda b_, h, i: (b_, i, h)),
        out_shape=jax.ShapeDtypeStruct((b, s, A_W), BF16),
        scratch_shapes=[pltpu.VMEM((2, 2 * LANES, 2 * tq), BF16),
                        pltpu.VMEM((1, 2 * tq), F32),
                        pltpu.VMEM((VT_ROWS, 2 * tq), F32),
                        pltpu.VMEM((2, tkc, 2 * tq), F32)],
        compiler_params=_cparams(("parallel", "parallel", "arbitrary")),
        name="diff_attn",
    )(a, qa_t, ka, va_t, kx, qx, diag, lam_vecs, subln_g.reshape(DIFF_V_DIM, 1))


def _dil_attn_kernel(bias_ref, q_ref, kp_ref, kc_ref, kn_ref, vp_ref, vc_ref, vn_ref,
                     o_ref, lse_ref, s_ref, *, n_sub, blk, qt):
    i = pl.program_id(2)
    nsb = qt // blk
    col = lax.broadcasted_iota(jnp.int32, (1, 3 * blk), 1)
    lane = lax.broadcasted_iota(jnp.int32, (blk, LANES), 1)
    first = lane < HEAD_DIM
    nt = (((1,), (1,)), ((), ()))

    def window(sb, prev_ref, cur_ref, next_ref, cols):
        parts = [prev_ref[0, :, cols] if sb == 0 else cur_ref[0, (sb - 1) * blk:sb * blk, cols],
                 cur_ref[0, sb * blk:(sb + 1) * blk, cols],
                 next_ref[0, :, cols] if sb == nsb - 1 else cur_ref[0, (sb + 1) * blk:(sb + 2) * blk, cols]]
        return jnp.concatenate(parts, axis=0)

    def scores(sb, hp, slot):
        cols = slice(hp * LANES, (hp + 1) * LANES)
        q = q_ref[0, sb * blk:(sb + 1) * blk, cols]
        zq = jnp.zeros_like(q)
        q2 = jnp.concatenate([jnp.where(first, q, zq), jnp.where(first, zq, q)], axis=0)
        s_ref[slot] = lax.dot_general(q2, window(sb, kp_ref, kc_ref, kn_ref, cols), nt,
                                      preferred_element_type=F32)

    def finish(sb, hp, slot):
        cols = slice(hp * LANES, (hp + 1) * LANES)
        s = s_ref[slot] + jnp.concatenate([bias_ref[2 * hp], bias_ref[2 * hp + 1]], axis=0)
        if sb == 0 or sb == nsb - 1:
            kidx = i * qt + (sb - 1) * blk + col
            s = jnp.where((kidx >= 0) & (kidx < n_sub), s, NEG_INF)
        m = jnp.max(s, axis=1, keepdims=True)
        p = jnp.exp(s - m)
        l = jnp.sum(p, axis=1, keepdims=True)
        o = jnp.dot(p.astype(BF16), window(sb, vp_ref, vc_ref, vn_ref, cols), preferred_element_type=F32) / l
        lse = jnp.broadcast_to(m + jnp.log(l), (2 * blk, LANES))
        o_ref[0, sb * blk:(sb + 1) * blk, cols] = jnp.where(first, o[0:blk], o[blk:]).astype(BF16)
        lse_ref[0, sb * blk:(sb + 1) * blk, cols] = jnp.where(first, lse[0:blk], lse[blk:])

    units = [(sb, hp) for sb in range(nsb) for hp in range(DIL_HEADS // 2)]
    nslot = s_ref.shape[0]
    for n in range(min(nslot - 1, len(units))):
        scores(*units[n], n % nslot)
    for n, unit in enumerate(units):
        if n + nslot - 1 < len(units):
            scores(*units[n + nslot - 1], (n + nslot - 1) % nslot)
        finish(*unit, n % nslot)


def _dil_attn(q2, k2, v2, window, dil, blk=128, qt=512):
    shape2 = q2.shape
    b, n_sub, w = shape2[0], shape2[1], shape2[2] // dil
    qt = min(qt, n_sub)
    nq = n_sub // qt
    nblk = n_sub // blk
    half = window // (2 * dil)
    slopes = 2.0 ** (-8.0 * np.arange(1, DIL_HEADS + 1) / DIL_HEADS)
    joff = np.arange(3 * blk)[None, :] - blk - np.arange(blk)[:, None]
    band = np.abs(joff) <= half
    bias = np.where(band[None], -slopes[:, None, None] * (np.abs(joff) * dil)[None], -np.inf).astype(np.float32)
    cur = lambda b_, r, i: (b_, i, r)
    prv = lambda b_, r, i: (b_, jnp.maximum(i * (qt // blk) - 1, 0), r)
    nxt = lambda b_, r, i: (b_, jnp.minimum((i + 1) * (qt // blk), nblk - 1), r)
    big = lambda f: pl.BlockSpec((1, qt, w), f)
    halo = lambda f: pl.BlockSpec((1, blk, w), f)
    kern = functools.partial(_dil_attn_kernel, n_sub=n_sub, blk=blk, qt=qt)
    o, lse = pl.pallas_call(
        kern,
        grid=(b, dil, nq),
        in_specs=[pl.BlockSpec((DIL_HEADS, blk, 3 * blk), lambda b_, r, i: (0, 0, 0)),
                  big(cur), halo(prv), big(cur), halo(nxt), halo(prv), big(cur), halo(nxt)],
        out_specs=[big(cur), big(cur)],
        out_shape=[jax.ShapeDtypeStruct(shape2, BF16), jax.ShapeDtypeStruct(shape2, F32)],
        scratch_shapes=[pltpu.VMEM((3, 2 * blk, 3 * blk), F32)],
        compiler_params=_cparams(("parallel", "parallel", "parallel")),
        name=f"dil_attn_d{dil}",
    )(jnp.asarray(bias), q2, k2, k2, k2, v2, v2, v2)
    return o, lse


def _out_proj_kernel(x_ref, oa_ref, o1_ref, o2_ref, o3_ref, l1_ref, l2_ref, l3_ref,
                     wa_ref, wb_ref, g_ref, x1_ref, xn_ref, slab_ref):
    tm = x_ref.shape[1]
    nsl = B_W // LANES

    def natural(ref, dil, base):
        if dil == 1:
            return ref[0].astype(F32)
        for r in range(dil):
            for sl in range(nsl):
                slab_ref[base + sl, pl.ds(r, tm // dil, stride=dil), :] = (
                    ref[0, :, r * B_W + sl * LANES:r * B_W + (sl + 1) * LANES].astype(F32))
        return jnp.concatenate([slab_ref[base + sl] for sl in range(nsl)], axis=1)

    dils = [dil for _, dil in DIL_PATTERNS]
    l1, l2, l3 = [natural(ref, dil, (2 * n) * nsl) for n, (ref, dil) in enumerate(zip((l1_ref, l2_ref, l3_ref), dils))]
    o1, o2, o3 = [natural(ref, dil, (2 * n + 1) * nsl) for n, (ref, dil) in enumerate(zip((o1_ref, o2_ref, o3_ref), dils))]
    m = jnp.maximum(jnp.maximum(l1, l2), l3)
    e1, e2, e3 = jnp.exp(l1 - m), jnp.exp(l2 - m), jnp.exp(l3 - m)
    ob = (e1 * o1 + e2 * o2 + e3 * o3) / (e1 + e2 + e3)
    y = (x_ref[0]
         + jnp.dot(oa_ref[0], wa_ref[...], preferred_element_type=F32)
         + jnp.dot(ob.astype(BF16), wb_ref[...], preferred_element_type=F32))
    x1_ref[0] = y
    xn_ref[0] = _rms(y, g_ref[...]).astype(BF16)


def _out_proj(x, oa, obs, lses, w_out, g, tm=512):
    b, s, d = x.shape
    wa = w_out[0:A_W].astype(BF16)
    wb = w_out[A_W:].astype(BF16)
    tok = lambda i, j: (i, j, 0)
    const = lambda i, j: (0, 0)
    dil_specs = [pl.BlockSpec((1, tm // dil, dil * B_W), tok) for _, dil in DIL_PATTERNS]
    return pl.pallas_call(
        _out_proj_kernel,
        grid=(b, s // tm),
        in_specs=[pl.BlockSpec((1, tm, d), tok), pl.BlockSpec((1, tm, A_W), tok)] + dil_specs + dil_specs +
                 [pl.BlockSpec((A_W, d), const), pl.BlockSpec((B_W, d), const), pl.BlockSpec((1, d), const)],
        out_specs=[pl.BlockSpec((1, tm, d), tok), pl.BlockSpec((1, tm, d), tok)],
        out_shape=[jax.ShapeDtypeStruct((b, s, d), F32), jax.ShapeDtypeStruct((b, s, d), BF16)],
        scratch_shapes=[pltpu.VMEM((2 * len(DIL_PATTERNS) * B_W // LANES, tm, LANES), F32)],
        compiler_params=_cparams(("parallel", "parallel")),
        name="out_proj",
    )(x, oa, *obs, *lses, wa, wb, g.reshape(1, d))


W_ROWS = N_KEYS // 2
I1_GROUP = 8
PEER_N1 = 16


def _cand_tables():
    groups = [(0, 16), (1, 8)] + [(a, 8) for a in range(2, 8)]
    a_idx, b_idx = [], []
    for a, n in groups:
        a_idx += [a] * n
        b_idx += list(range(n))
    a_idx += list(range(8, 16))
    b_idx += [0] * 8
    a_idx, b_idx = np.array(a_idx), np.array(b_idx)
    ok = (a_idx + 1) * (b_idx + 1) <= PEER_TOPK
    pos = np.where(ok, a_idx * PEER_TOPK + b_idx, -1)
    return groups, pos.astype(np.int32)


def _batcher_pairs(n):
    pairs = []
    p = 1
    while p < n:
        k = p
        while k >= 1:
            for j in range(k % p, n - k, 2 * k):
                for i in range(min(k, n - j - k)):
                    if (i + j) // (2 * p) == (i + j + k) // (2 * p):
                        pairs.append((i + j, i + j + k))
            k //= 2
        p *= 2
    return pairs


def _top_rows_sorted(s, k, rank=None):
    n, t = s.shape
    if t > LANES:
        parts = [_top_rows_sorted(s[:, c:c + LANES], k, None if rank is None else rank[:, c:c + LANES])
                 for c in range(0, t, LANES)]
        return tuple(jnp.concatenate(z, axis=1) for z in zip(*parts))
    sl = 8
    ng = n // sl
    x = [s[g * sl:(g + 1) * sl] for g in range(ng)]
    if rank is None:
        sub = lax.broadcasted_iota(jnp.int32, (sl, t), 0)
        r = [sub + g * sl for g in range(ng)]
    else:
        r = [rank[g * sl:(g + 1) * sl] for g in range(ng)]

    def before(xa, ra, xb, rb):
        return (xa > xb) | ((xa == xb) & (ra < rb))

    for a, b in [ab for ab in _batcher_pairs(1 << (ng - 1).bit_length()) if ab[1] < ng]:
        keep = before(x[a], r[a], x[b], r[b])
        x[a], x[b] = jnp.where(keep, x[a], x[b]), jnp.where(keep, x[b], x[a])
        r[a], r[b] = jnp.where(keep, r[a], r[b]), jnp.where(keep, r[b], r[a])

    vals, idxs = [], []
    for p in range(k):
        wv, wr = x[0], r[0]
        for shift in (4, 2, 1):
            pv, pr = pltpu.roll(wv, shift, 0), pltpu.roll(wr, shift, 0)
            keep = before(wv, wr, pv, pr) | (wr == pr)
            wv, wr = jnp.where(keep, wv, pv), jnp.where(keep, wr, pr)
        vals.append(wv[0:1])
        idxs.append(wr[0:1])
        popped = r[0] == wr
        for d in range(min(ng, k - p - 1)):
            if d + 1 < ng:
                x[d] = jnp.where(popped, x[d + 1], x[d])
                r[d] = jnp.where(popped, r[d + 1], r[d])
            else:
                x[d] = jnp.where(popped, NEG_INF, x[d])
    return jnp.concatenate(vals, axis=0), jnp.concatenate(idxs, axis=0)


def _peer_select_kernel(xn_ref, wq_t_ref, keys_ref, pos_ref, w_ref,
                        q_ref, v_ref, ix_ref, e_ref, gt_ref, i1_ref, i2_ref, g_ref, *, groups):
    nt = (((1,), (1,)), ((), ()))
    tt = xn_ref.shape[0]

    @pl.when(pl.program_id(0) == 0)
    def _no_previous_tile():
        for ref in (i1_ref, i2_ref, g_ref):
            ref[...] = jnp.zeros(ref.shape, F32)

    q_ref[...] = lax.dot_general(wq_t_ref[...], xn_ref[...], nt,
                                 preferred_element_type=F32).astype(BF16)

    row = lax.broadcasted_iota(jnp.int32, (N_KEYS, N_KEYS), 0)
    pr = row & (W_ROWS - 1)
    half = I1_GROUP // 2
    i1_of_row = ((pr // half) * I1_GROUP + (pr % half) + half * (row // W_ROWS)).astype(F32)
    sub = row.astype(F32)
    hi_mask = jnp.uint32(0xFFFF0000)
    nch, rpc = w_ref.shape[0], W_ROWS // w_ref.shape[0]

    def gate_words(t):
        r1 = i1_ref[pl.ds(t, 1), :]
        r2 = i2_ref[pl.ds(t, 1), :]
        gg = g_ref[pl.ds(t, 1), :]
        lt = jnp.where(i1_of_row == r1, 1.0, 0.0).astype(BF16)
        rt = jnp.where(sub == r2, gg, 0.0).astype(BF16)
        w = lax.dot_general(lt, rt, nt, preferred_element_type=F32)
        bits = lax.bitcast_convert_type(w.astype(BF16).astype(F32), jnp.uint32)
        words = (bits[W_ROWS:] & hi_mask) | (bits[0:W_ROWS] >> 16)
        w_ref[:, pl.ds(pl.multiple_of(t * rpc, rpc), rpc), :] = words.reshape(nch, rpc, N_KEYS)

    per_half = tt // (2 * PEER_HEADS)

    def half_topk(hc, carry):
        qh = q_ref[pl.ds(pl.multiple_of(hc * PEER_HALF, PEER_HALF), PEER_HALF), :]
        s = jnp.dot(keys_ref[hc % 2], qh, preferred_element_type=F32)
        for u in range(per_half):
            gate_words(hc * per_half + u)
        vals, idxs = _top_rows_sorted(s, PEER_TOPK)
        v_ref[hc] = vals
        ix_ref[hc] = idxs
        return carry

    lax.fori_loop(0, 2 * PEER_HEADS, half_topk, 0, unroll=2)

    pos_tab = pos_ref[...]

    def head_select(h, carry):
        v1, v2 = v_ref[2 * h], v_ref[2 * h + 1]
        i1, i2 = ix_ref[2 * h], ix_ref[2 * h + 1]
        v1g, v2g, i1g, i2g = [], [], [], []
        for a, n in groups:
            v1g.append(jnp.broadcast_to(v1[a:a + 1], (n, tt)))
            i1g.append(jnp.broadcast_to(i1[a:a + 1], (n, tt)))
            v2g.append(v2[0:n])
            i2g.append(i2[0:n])
        v1g.append(v1[8:16]); i1g.append(i1[8:16])
        v2g.append(jnp.broadcast_to(v2[0:1], (8, tt))); i2g.append(jnp.broadcast_to(i2[0:1], (8, tt)))
        cand = jnp.concatenate(v1g, axis=0) + jnp.concatenate(v2g, axis=0)
        eidx = jnp.concatenate(i1g, axis=0) * N_KEYS + jnp.concatenate(i2g, axis=0)
        cand = jnp.where(pos_tab >= 0, cand, NEG_INF)
        key = pos_tab * (N_KEYS * N_KEYS) + eidx
        top, sel = _top_rows_sorted(cand, PEER_TOPK, rank=key)
        ek = sel & (N_KEYS * N_KEYS - 1)
        ex = jnp.exp(top - top[0:1])
        gate = ex / jnp.sum(ex, axis=0, keepdims=True)
        e_ref[h] = ek.astype(F32)
        gt_ref[h] = gate
        return carry

    lax.fori_loop(0, PEER_HEADS, head_select, 0, unroll=2)

    ef = e_ref[...].reshape(PEER_HEADS * PEER_TOPK, tt)
    i1f = jnp.floor(ef * (1.0 / N_KEYS))
    i1_ref[...] = i1f.T
    i2_ref[...] = (ef - i1f * N_KEYS).T
    g_ref[...] = gt_ref[...].reshape(PEER_HEADS * PEER_TOPK, tt).T


def _peer_select(xn, wq, keys1, keys2, n1, tt=256):
    t, d = xn.shape
    groups, pos = _cand_tables()
    pos_tab = jnp.broadcast_to(jnp.asarray(pos)[:, None], (pos.shape[0], tt))
    nslot = PEER_HEADS * PEER_TOPK
    qw = 2 * PEER_HEADS * PEER_HALF
    nch = N_KEYS // n1
    rpc = W_ROWS // nch
    const = lambda i: (0, 0)
    kern = functools.partial(_peer_select_kernel, groups=groups)
    ntile = t // tt
    return pl.pallas_call(
        kern,
        grid=(ntile + 1,),
        in_specs=[pl.BlockSpec((tt, d), lambda i: (jnp.minimum(i, ntile - 1), 0)),
                  pl.BlockSpec((qw, d), const),
                  pl.BlockSpec((2, N_KEYS, PEER_HALF), lambda i: (0, 0, 0)),
                  pl.BlockSpec((pos.shape[0], tt), const)],
        out_specs=pl.BlockSpec((nch, tt * rpc, N_KEYS), lambda i: (0, jnp.maximum(i - 1, 0), 0)),
        out_shape=jax.ShapeDtypeStruct((nch, t * rpc, N_KEYS), jnp.uint32),
        scratch_shapes=[pltpu.VMEM((qw, tt), BF16),
                        pltpu.VMEM((2 * PEER_HEADS, PEER_TOPK, tt), F32),
                        pltpu.VMEM((2 * PEER_HEADS, PEER_TOPK, tt), jnp.int32),
                        pltpu.VMEM((PEER_HEADS, PEER_TOPK, tt), F32),
                        pltpu.VMEM((PEER_HEADS, PEER_TOPK, tt), F32),
                        pltpu.VMEM((tt, nslot), F32),
                        pltpu.VMEM((tt, nslot), F32),
                        pltpu.VMEM((tt, nslot), F32)],
        compiler_params=_cparams(("arbitrary",)),
        name="peer_select",
    )(xn, wq.T.astype(BF16), jnp.stack([keys1, keys2]).astype(BF16), pos_tab)


def _gelu(x):
    c = math.sqrt(2.0 / math.pi)
    return 0.5 * x * (1.0 + jnp.tanh(c * (x + 0.044715 * (x * x * x))))


def _peer_experts_kernel(x1_ref, xn_ref, w_ref, u_ref, v_ref, fg_ref, y_ref, acc_ref, *, tt, n1):
    c = pl.program_id(1)
    nc = pl.num_programs(1)
    nt = (((1,), (1,)), ((), ()))
    hi_mask = jnp.uint32(0xFFFF0000)
    half = I1_GROUP // 2
    rpc = n1 // 2

    @pl.when(c == 0)
    def _init():
        acc_ref[...] = jnp.zeros(acc_ref.shape, F32)

    a = lax.dot_general(xn_ref[...], u_ref[...], nt, preferred_element_type=F32)
    lo, hi = [], []
    for g in range(n1 // I1_GROUP):
        for u in range(half):
            words = w_ref[0, pl.ds(g * half + u, tt, stride=rpc), :]
            lo.append(lax.bitcast_convert_type(words << 16, F32))
            hi.append(lax.bitcast_convert_type(words & hi_mask, F32))
    wc = jnp.concatenate(
        [part[g * half + u] for g in range(n1 // I1_GROUP) for part in (lo, hi) for u in range(half)], axis=1)
    z = (_gelu(a) * wc).astype(BF16)
    acc_ref[...] += jnp.dot(z, v_ref[...], preferred_element_type=F32)

    @pl.when(c == nc - 1)
    def _fin():
        y_ref[...] = _rms(x1_ref[...] + acc_ref[...], fg_ref[...])


def _peer_experts(x1, xn, words, u_tab, v_tab, final_g, n1, tt=512):
    t, d = x1.shape
    ne = u_tab.shape[0]
    ec = n1 * N_KEYS
    rpc = n1 // 2
    tok = lambda i, c: (i, 0)
    kern = functools.partial(_peer_experts_kernel, tt=tt, n1=n1)
    return pl.pallas_call(
        kern,
        grid=(t // tt, ne // ec),
        in_specs=[pl.BlockSpec((tt, d), tok), pl.BlockSpec((tt, d), tok),
                  pl.BlockSpec((1, tt * rpc, N_KEYS), lambda i, c: (c, i, 0)),
                  pl.BlockSpec((ec, d), lambda i, c: (c, 0)),
                  pl.BlockSpec((ec, d), lambda i, c: (c, 0)),
                  pl.BlockSpec((1, d), lambda i, c: (0, 0))],
        out_specs=pl.BlockSpec((tt, d), tok),
        out_shape=jax.ShapeDtypeStruct((t, d), F32),
        scratch_shapes=[pltpu.VMEM((tt, d), F32)],
        compiler_params=_cparams(("parallel", "arbitrary")),
        name="peer_experts",
    )(x1, xn, words, u_tab, v_tab, final_g.reshape(1, d))


def _trunk(x, norm1_g, w_in, lam_vecs, subln_g, w_out, norm2_g, wq, keys1, keys2, u_bf, v_bf, final_g):
    b, s, d = x.shape
    qa_t, ka, va_t, qkv = _in_proj(x, norm1_g, w_in)
    oa = _diff_attn(qa_t, ka, va_t, lam_vecs, subln_g)
    obs, lses = [], []
    for (window, dil), (q2, k2, v2) in zip(DIL_PATTERNS, qkv):
        o, lse = _dil_attn(q2, k2, v2, window, dil)
        obs.append(o)
        lses.append(lse)
    x1, xn2 = _out_proj(x, oa, obs, lses, w_out, norm2_g)
    x1, xn2 = x1.reshape(b * s, d), xn2.reshape(b * s, d)
    words = _peer_select(xn2, wq, keys1, keys2, PEER_N1)
    y = _peer_experts(x1, xn2, words, u_bf, v_bf, final_g, PEER_N1)
    return y.reshape(b, s, d)


def kernel(x_prompt, x_sample, norm1_g, w_in, lambda_q1, lambda_k1, lambda_q2, lambda_k2, subln_g, w_out,
           norm2_g, peer_wq, peer_keys1, peer_keys2, peer_u, peer_v, final_g):
    lam_vecs = jnp.concatenate([lambda_q1, lambda_k1, lambda_q2, lambda_k2], axis=0).astype(F32)
    u_bf = peer_u[0].astype(BF16)
    v_bf = peer_v[0].astype(BF16)
    args = (norm1_g[0], w_in[0], lam_vecs, subln_g[0], w_out[0], norm2_g[0], peer_wq[0],
            peer_keys1[0], peer_keys2[0], u_bf, v_bf, final_g)
    return (_trunk(x_prompt, *args), _trunk(x_sample, *args))
```

```python
import functools
import math

import jax
import jax.numpy as jnp
import numpy as np
from jax import lax
from jax.experimental import pallas as pl
from jax.experimental.pallas import tpu as pltpu

D_MODEL = 1024
HEAD_DIM = 64
DIFF_HEADS = 4
DIFF_V_DIM = 2 * HEAD_DIM
DIL_HEADS = 8
DIL_PATTERNS = ((128, 1), (512, 4), (2048, 16))
A_W = DIFF_HEADS * 2 * HEAD_DIM
B_W = DIL_HEADS * HEAD_DIM
PEER_HEADS = 8
N_KEYS = 128
PEER_HALF = 128
PEER_TOPK = 16
EPS = 1e-6
LAMBDA_INIT = 0.8 - 0.6 * math.exp(-0.3 * 0)

LOG2E = math.log2(math.e)
ATT_CHUNK = 256
VT_ROWS = DIFF_V_DIM + 16
LANES = 128
VMEM_LIMIT = 56 * 1024 * 1024

F32 = jnp.float32
BF16 = jnp.bfloat16
NEG_INF = float("-inf")


def _cparams(sem):
    return pltpu.CompilerParams(dimension_semantics=sem, vmem_limit_bytes=VMEM_LIMIT)


def _rms(x, g):
    return x * lax.rsqrt(jnp.mean(x * x, axis=-1, keepdims=True) + EPS) * g


def _in_proj_kernel(x_ref, g_ref, wqa_t_ref, wva_t_ref, wrest_ref, qa_t_ref, ka_ref, va_t_ref, *rest_refs, tkc):
    dil_refs, slab_ref = rest_refs[:-1], rest_refs[-1]
    xn = _rms(x_ref[0], g_ref[...]).astype(BF16)
    tm = xn.shape[0]
    nt = (((1,), (1,)), ((), ()))
    qa_t_ref[0] = lax.dot_general(wqa_t_ref[...], xn, nt, preferred_element_type=F32).astype(BF16)
    va_t = lax.dot_general(wva_t_ref[...], xn, nt, preferred_element_type=F32).astype(BF16)
    ones = jnp.ones((VT_ROWS - DIFF_V_DIM, tkc), BF16)
    for hd in range(DIFF_HEADS):
        for c in range(tm // tkc):
            va_t_ref[0, hd, c, 0:DIFF_V_DIM, :] = va_t[hd * DIFF_V_DIM:(hd + 1) * DIFF_V_DIM, c * tkc:(c + 1) * tkc]
            va_t_ref[0, hd, c, DIFF_V_DIM:VT_ROWS, :] = ones
    rest = jnp.dot(xn, wrest_ref[...], preferred_element_type=F32)
    ka_ref[0] = rest[:, 0:A_W].astype(BF16)
    nsl = B_W // LANES
    for a in range(3):
        cols = rest[:, (1 + a) * B_W:(2 + a) * B_W]
        dil_refs[a][0] = cols.astype(BF16)
        for sl in range(nsl):
            slab_ref[a * nsl + sl] = cols[:, sl * LANES:(sl + 1) * LANES]
    for pi, (_, dil) in enumerate(DIL_PATTERNS[1:]):
        for a in range(3):
            out = dil_refs[3 * (pi + 1) + a]
            for r in range(dil):
                for sl in range(nsl):
                    rows = slab_ref[a * nsl + sl, pl.ds(r, tm // dil, stride=dil), :]
                    out[0, :, r * B_W + sl * LANES:r * B_W + (sl + 1) * LANES] = rows.astype(BF16)


def _in_proj(x, g, w_in, tm=512, tkc=ATT_CHUNK):
    b, s, d = x.shape
    assert DIL_PATTERNS[0][1] == 1
    scale = HEAD_DIM ** -0.5
    wqa_t = (w_in[:, 0:A_W] * (scale * LOG2E)).T.astype(BF16)
    wva_t = w_in[:, 2 * A_W:3 * A_W].T.astype(BF16)
    wrest = jnp.concatenate(
        [w_in[:, A_W:2 * A_W], w_in[:, 3 * A_W:4 * A_W] * scale, w_in[:, 4 * A_W:]], axis=1).astype(BF16)
    tok = lambda i, j: (i, j, 0)
    tr = lambda i, j: (i, 0, j)
    const = lambda i, j: (0, 0)
    dil_specs, dil_shapes = [], []
    for _, dil in DIL_PATTERNS:
        dil_specs += [pl.BlockSpec((1, tm // dil, dil * B_W), tok)] * 3
        dil_shapes += [jax.ShapeDtypeStruct((b, s // dil, dil * B_W), BF16)] * 3
    outs = pl.pallas_call(
        functools.partial(_in_proj_kernel, tkc=tkc),
        grid=(b, s // tm),
        in_specs=[pl.BlockSpec((1, tm, d), tok),
                  pl.BlockSpec((1, d), const),
                  pl.BlockSpec((A_W, d), const),
                  pl.BlockSpec((A_W, d), const),
                  pl.BlockSpec((d, 4 * A_W), const)],
        out_specs=[pl.BlockSpec((1, A_W, tm), tr),
                   pl.BlockSpec((1, tm, A_W), tok),
                   pl.BlockSpec((1, DIFF_HEADS, tm // tkc, VT_ROWS, tkc), lambda i, j: (i, 0, j, 0, 0))] + dil_specs,
        out_shape=[jax.ShapeDtypeStruct((b, A_W, s), BF16),
                   jax.ShapeDtypeStruct((b, s, A_W), BF16),
                   jax.ShapeDtypeStruct((b, DIFF_HEADS, s // tkc, VT_ROWS, tkc), BF16)] + dil_shapes,
        scratch_shapes=[pltpu.VMEM((3 * B_W // LANES, tm, LANES), F32)],
        compiler_params=_cparams(("parallel", "parallel")),
        name="in_proj",
    )(x, g.reshape(1, d), wqa_t, wva_t, wrest)
    qkv = [tuple(outs[3 + 3 * p:6 + 3 * p]) for p in range(len(DIL_PATTERNS))]
    return outs[0], outs[1], outs[2], qkv


def _bf16_split3(x):
    x = np.asarray(x, np.float32)
    pieces = []
    for _ in range(3):
        p = x.astype(jnp.bfloat16).astype(np.float32)
        pieces.append(p.astype(np.float64))
        x = (x - p).astype(np.float32)
    return pieces


def _alibi_tables(tq, tkc):
    slopes = 2.0 ** (-8.0 * np.arange(1, DIFF_HEADS + 1) / DIFF_HEADS)
    a = (slopes * LOG2E).astype(np.float32)
    kx = np.zeros((tkc, LANES), np.float32)
    s_rel = np.arange(tkc, dtype=np.float32)
    s_lo = s_rel % 256
    kx[:, 0:3] = s_lo[:, None]
    kx[:, 3:6] = (s_rel - s_lo)[:, None]
    kx[:, 6:9] = 1.0
    qx = np.zeros((DIFF_HEADS, 2, LANES, tq), np.float64)
    t_rel = np.arange(tq, dtype=np.float32)
    for hd in range(DIFF_HEADS):
        a3 = _bf16_split3(a[hd])
        f3 = _bf16_split3(-(a[hd] * t_rel).astype(np.float32))
        for r in range(3):
            qx[hd, 0, r, :] = a3[r]
            qx[hd, 0, 3 + r, :] = a3[r]
            qx[hd, 0, 6 + r, :] = f3[r]
    qx[:, 1] = -qx[:, 0]
    rel = s_rel[:, None] - t_rel[None, :]
    ndiag = tq // tkc
    diag = np.stack([np.stack([-(a[hd] * np.abs(rel + d * tkc)) for d in range(ndiag)])
                     for hd in range(DIFF_HEADS)]).astype(np.float32)
    return (jnp.asarray(a), jnp.asarray(kx, F32).astype(BF16), jnp.asarray(qx, F32).astype(BF16),
            jnp.asarray(diag))


def _diff_attn_kernel(a_ref, qt_ref, k_ref, vt_ref, kx_ref, qx_ref, diag_ref, lam_ref, sg_ref, o_ref,
                      qs_ref, m_ref, acc_ref, s_ref, *, tq, tkc, unroll):
    h = pl.program_id(1)
    i = pl.program_id(2)
    nk = k_ref.shape[1] // tkc
    ndiag = tq // tkc
    a_h = a_ref[h]

    qt = qt_ref[0]
    zero = jnp.zeros((HEAD_DIM, tq), BF16)
    for side in range(2):
        qs_ref[side, 0:HEAD_DIM, 0:tq] = qt[0:HEAD_DIM]
        qs_ref[side, 0:HEAD_DIM, tq:2 * tq] = zero
        qs_ref[side, HEAD_DIM:DIFF_V_DIM, 0:tq] = zero
        qs_ref[side, HEAD_DIM:DIFF_V_DIM, tq:2 * tq] = qt[HEAD_DIM:]
        qs_ref[side, DIFF_V_DIM:, 0:tq] = qx_ref[0, side]
        qs_ref[side, DIFF_V_DIM:, tq:2 * tq] = qx_ref[0, side]
    m_ref[...] = jnp.full(m_ref.shape, NEG_INF, F32)
    acc_ref[...] = jnp.zeros(acc_ref.shape, F32)

    def update(j, s, c):
        m_old = m_ref[...]
        m_new = jnp.maximum(m_old, jnp.max(s, axis=0, keepdims=True) + c)
        alpha = jnp.exp2(m_old - m_new)
        p = jnp.exp2(s - (m_new - c)).astype(BF16)
        acc_ref[...] = alpha * acc_ref[...] + jnp.dot(vt_ref[0, 0, j], p, preferred_element_type=F32)
        m_ref[...] = m_new

    n_left = i * ndiag

    def scores_diag(d, slot):
        k0 = pl.multiple_of((n_left + d) * tkc, tkc)
        s_ref[slot] = jnp.dot(k_ref[0, pl.ds(k0, tkc), :], qs_ref[0, 0:DIFF_V_DIM, :],
                              preferred_element_type=F32)

    def consume_diag(d, slot):
        bias = diag_ref[0, d]
        update(n_left + d, s_ref[slot] + jnp.concatenate([bias, bias], axis=1), jnp.float32(0.0))

    cnt = nk - ndiag
    last = cnt - 1

    def chunk_of(n):
        n = jnp.minimum(n, last)
        side = (n >= n_left).astype(jnp.int32)
        j = n + side * ndiag
        dist = (1 - 2 * side) * (i * tq - j * tkc)
        return j, side, -a_h * dist.astype(F32)

    def scores(n, slot):
        j, side, _ = chunk_of(n)
        k0 = pl.multiple_of(j * tkc, tkc)
        lhs = jnp.concatenate([k_ref[0, pl.ds(k0, tkc), :], kx_ref[...]], axis=1)
        s_ref[slot] = jnp.dot(lhs, qs_ref[side], preferred_element_type=F32)

    def consume(n, slot):
        j, _, c = chunk_of(n)
        update(j, s_ref[slot], c)

    npeel = cnt % unroll
    head = [(scores_diag, consume_diag, d) for d in range(ndiag)] + [(scores, consume, n) for n in range(npeel)]
    assert len(head) % 2 == 0 and unroll % 2 == 0
    head[0][0](head[0][2], 0)
    for t, (_, cons, arg) in enumerate(head):
        if t + 1 < len(head):
            head[t + 1][0](head[t + 1][2], (t + 1) % 2)
        elif npeel < cnt:
            scores(npeel, 0)
        cons(arg, t % 2)

    def group(g, carry):
        for u in range(unroll):
            scores(npeel + g * unroll + u + 1, (u + 1) % 2)
            consume(npeel + g * unroll + u, u % 2)
        return carry

    lax.fori_loop(0, (cnt - npeel) // unroll, group, 0)

    lam_v = lam_ref[...]
    lam = (jnp.exp(jnp.sum(lam_v[0:1] * lam_v[1:2], axis=1, keepdims=True))
           - jnp.exp(jnp.sum(lam_v[2:3] * lam_v[3:4], axis=1, keepdims=True)) + LAMBDA_INIT)
    acc = acc_ref[...]
    o = acc[0:DIFF_V_DIM] / acc[DIFF_V_DIM:DIFF_V_DIM + 1]
    o = o[:, 0:tq] - lam * o[:, tq:2 * tq]
    ms = jnp.mean(o * o, axis=0, keepdims=True)
    o = o * lax.rsqrt(ms + EPS) * sg_ref[...] * (1.0 - LAMBDA_INIT)
    o_ref[0] = o.T.astype(BF16)


def _diff_attn(qa_t, ka, va_t, lam_vecs, subln_g, tq=512, tkc=ATT_CHUNK, unroll=14):
    b, _, s = qa_t.shape
    nk = s // tkc
    a, kx, qx, diag = _alibi_tables(tq, tkc)
    kern = functools.partial(_diff_attn_kernel, tq=tq, tkc=tkc, unroll=unroll)
    c2 = lambda b_, h, i: (0, 0)
    return pl.pallas_call(
        kern,
        grid=(b, DIFF_HEADS, s // tq),
        in_specs=[pl.BlockSpec(memory_space=pltpu.SMEM),
                  pl.BlockSpec((1, DIFF_V_DIM, tq), lambda b_, h, i: (b_, h, i)),
                  pl.BlockSpec((1, s, DIFF_V_DIM), lambda b_, h, i: (b_, 0, h)),
                  pl.BlockSpec((1, 1, nk, VT_ROWS, tkc), lambda b_, h, i: (b_, h, 0, 0, 0)),
                  pl.BlockSpec((tkc, LANES), c2),
                  pl.BlockSpec((1, 2, LANES, tq), lambda b_, h, i: (h, 0, 0, 0)),
                  pl.BlockSpec((1, tq // tkc, tkc, tq), lambda b_, h, i: (h, 0, 0, 0)),
                  pl.BlockSpec((4, HEAD_DIM), c2),
                  pl.BlockSpec((DIFF_V_DIM, 1), c2)],
        out_specs=pl.BlockSpec((1, tq, DIFF_V_DIM), lambda b_, h, i: (b_, i, h)),
        out_shape=jax.ShapeDtypeStruct((b, s, A_W), BF16),
        scratch_shapes=[pltpu.VMEM((2, 2 * LANES, 2 * tq), BF16),
                        pltpu.VMEM((1, 2 * tq), F32),
                        pltpu.VMEM((VT_ROWS, 2 * tq), F32),
                        pltpu.VMEM((2, tkc, 2 * tq), F32)],
        compiler_params=_cparams(("parallel", "parallel", "arbitrary")),
        name="diff_attn",
    )(a, qa_t, ka, va_t, kx, qx, diag, lam_vecs, subln_g.reshape(DIFF_V_DIM, 1))


def _dil_attn_kernel(bias_ref, q_ref, kp_ref, kc_ref, kn_ref, vp_ref, vc_ref, vn_ref,
                     o_ref, lse_ref, s_ref, *, n_sub, blk, qt):
    i = pl.program_id(2)
    nsb = qt // blk
    col = lax.broadcasted_iota(jnp.int32, (1, 3 * blk), 1)
    lane = lax.broadcasted_iota(jnp.int32, (blk, LANES), 1)
    first = lane < HEAD_DIM
    nt = (((1,), (1,)), ((), ()))

    def window(sb, prev_ref, cur_ref, next_ref, cols):
        parts = [prev_ref[0, :, cols] if sb == 0 else cur_ref[0, (sb - 1) * blk:sb * blk, cols],
                 cur_ref[0, sb * blk:(sb + 1) * blk, cols],
                 next_ref[0, :, cols] if sb == nsb - 1 else cur_ref[0, (sb + 1) * blk:(sb + 2) * blk, cols]]
        return jnp.concatenate(parts, axis=0)

    def scores(sb, hp, slot):
        cols = slice(hp * LANES, (hp + 1) * LANES)
        q = q_ref[0, sb * blk:(sb + 1) * blk, cols]
        zq = jnp.zeros_like(q)
        q2 = jnp.concatenate([jnp.where(first, q, zq), jnp.where(first, zq, q)], axis=0)
        s_ref[slot] = lax.dot_general(q2, window(sb, kp_ref, kc_ref, kn_ref, cols), nt,
                                      preferred_element_type=F32)

    def finish(sb, hp, slot):
        cols = slice(hp * LANES, (hp + 1) * LANES)
        s = s_ref[slot] + jnp.concatenate([bias_ref[2 * hp], bias_ref[2 * hp + 1]], axis=0)
        if sb == 0 or sb == nsb - 1:
            kidx = i * qt + (sb - 1) * blk + col
            s = jnp.where((kidx >= 0) & (kidx < n_sub), s, NEG_INF)
        m = jnp.max(s, axis=1, keepdims=True)
        p = jnp.exp(s - m)
        l = jnp.sum(p, axis=1, keepdims=True)
        o = jnp.dot(p.astype(BF16), window(sb, vp_ref, vc_ref, vn_ref, cols), preferred_element_type=F32) / l
        lse = jnp.broadcast_to(m + jnp.log(l), (2 * blk, LANES))
        o_ref[0, sb * blk:(sb + 1) * blk, cols] = jnp.where(first, o[0:blk], o[blk:]).astype(BF16)
        lse_ref[0, sb * blk:(sb + 1) * blk, cols] = jnp.where(first, lse[0:blk], lse[blk:])

    units = [(sb, hp) for sb in range(nsb) for hp in range(DIL_HEADS // 2)]
    nslot = s_ref.shape[0]
    for n in range(min(nslot - 1, len(units))):
        scores(*units[n], n % nslot)
    for n, unit in enumerate(units):
        if n + nslot - 1 < len(units):
            scores(*units[n + nslot - 1], (n + nslot - 1) % nslot)
        finish(*unit, n % nslot)


def _dil_attn(q2, k2, v2, window, dil, blk=128, qt=512):
    shape2 = q2.shape
    b, n_sub, w = shape2[0], shape2[1], shape2[2] // dil
    qt = min(qt, n_sub)
    nq = n_sub // qt
    nblk = n_sub // blk
    half = window // (2 * dil)
    slopes = 2.0 ** (-8.0 * np.arange(1, DIL_HEADS + 1) / DIL_HEADS)
    joff = np.arange(3 * blk)[None, :] - blk - np.arange(blk)[:, None]
    band = np.abs(joff) <= half
    bias = np.where(band[None], -slopes[:, None, None] * (np.abs(joff) * dil)[None], -np.inf).astype(np.float32)
    cur = lambda b_, r, i: (b_, i, r)
    prv = lambda b_, r, i: (b_, jnp.maximum(i * (qt // blk) - 1, 0), r)
    nxt = lambda b_, r, i: (b_, jnp.minimum((i + 1) * (qt // blk), nblk - 1), r)
    big = lambda f: pl.BlockSpec((1, qt, w), f)
    halo = lambda f: pl.BlockSpec((1, blk, w), f)
    kern = functools.partial(_dil_attn_kernel, n_sub=n_sub, blk=blk, qt=qt)
    o, lse = pl.pallas_call(
        kern,
        grid=(b, dil, nq),
        in_specs=[pl.BlockSpec((DIL_HEADS, blk, 3 * blk), lambda b_, r, i: (0, 0, 0)),
                  big(cur), halo(prv), big(cur), halo(nxt), halo(prv), big(cur), halo(nxt)],
        out_specs=[big(cur), big(cur)],
        out_shape=[jax.ShapeDtypeStruct(shape2, BF16), jax.ShapeDtypeStruct(shape2, F32)],
        scratch_shapes=[pltpu.VMEM((3, 2 * blk, 3 * blk), F32)],
        compiler_params=_cparams(("parallel", "parallel", "parallel")),
        name=f"dil_attn_d{dil}",
    )(jnp.asarray(bias), q2, k2, k2, k2, v2, v2, v2)
    return o, lse


def _out_proj_kernel(x_ref, oa_ref, o1_ref, o2_ref, o3_ref, l1_ref, l2_ref, l3_ref,
                     wa_ref, wb_ref, g_ref, x1_ref, xn_ref, slab_ref):
    tm = x_ref.shape[1]
    nsl = B_W // LANES

    def natural(ref, dil, base):
        if dil == 1:
            return ref[0].astype(F32)
        for r in range(dil):
            for sl in range(nsl):
                slab_ref[base + sl, pl.ds(r, tm // dil, stride=dil), :] = (
                    ref[0, :, r * B_W + sl * LANES:r * B_W + (sl + 1) * LANES].astype(F32))
        return jnp.concatenate([slab_ref[base + sl] for sl in range(nsl)], axis=1)

    dils = [dil for _, dil in DIL_PATTERNS]
    l1, l2, l3 = [natural(ref, dil, (2 * n) * nsl) for n, (ref, dil) in enumerate(zip((l1_ref, l2_ref, l3_ref), dils))]
    o1, o2, o3 = [natural(ref, dil, (2 * n + 1) * nsl) for n, (ref, dil) in enumerate(zip((o1_ref, o2_ref, o3_ref), dils))]
    m = jnp.maximum(jnp.maximum(l1, l2), l3)
    e1, e2, e3 = jnp.exp(l1 - m), jnp.exp(l2 - m), jnp.exp(l3 - m)
    ob = (e1 * o1 + e2 * o2 + e3 * o3) / (e1 + e2 + e3)
    y = (x_ref[0]
         + jnp.dot(oa_ref[0], wa_ref[...], preferred_element_type=F32)
         + jnp.dot(ob.astype(BF16), wb_ref[...], preferred_element_type=F32))
    x1_ref[0] = y
    xn_ref[0] = _rms(y, g_ref[...]).astype(BF16)


def _out_proj(x, oa, obs, lses, w_out, g, tm=512):
    b, s, d = x.shape
    wa = w_out[0:A_W].astype(BF16)
    wb = w_out[A_W:].astype(BF16)
    tok = lambda i, j: (i, j, 0)
    const = lambda i, j: (0, 0)
    dil_specs = [pl.BlockSpec((1, tm // dil, dil * B_W), tok) for _, dil in DIL_PATTERNS]
    return pl.pallas_call(
        _out_proj_kernel,
        grid=(b, s // tm),
        in_specs=[pl.BlockSpec((1, tm, d), tok), pl.BlockSpec((1, tm, A_W), tok)] + dil_specs + dil_specs +
                 [pl.BlockSpec((A_W, d), const), pl.BlockSpec((B_W, d), const), pl.BlockSpec((1, d), const)],
        out_specs=[pl.BlockSpec((1, tm, d), tok), pl.BlockSpec((1, tm, d), tok)],
        out_shape=[jax.ShapeDtypeStruct((b, s, d), F32), jax.ShapeDtypeStruct((b, s, d), BF16)],
        scratch_shapes=[pltpu.VMEM((2 * len(DIL_PATTERNS) * B_W // LANES, tm, LANES), F32)],
        compiler_params=_cparams(("parallel", "parallel")),
        name="out_proj",
    )(x, oa, *obs, *lses, wa, wb, g.reshape(1, d))


W_ROWS = N_KEYS // 2
I1_GROUP = 8
PEER_N1 = 16


def _cand_tables():
    groups = [(0, 16), (1, 8)] + [(a, 8) for a in range(2, 8)]
    a_idx, b_idx = [], []
    for a, n in groups:
        a_idx += [a] * n
        b_idx += list(range(n))
    a_idx += list(range(8, 16))
    b_idx += [0] * 8
    a_idx, b_idx = np.array(a_idx), np.array(b_idx)
    ok = (a_idx + 1) * (b_idx + 1) <= PEER_TOPK
    pos = np.where(ok, a_idx * PEER_TOPK + b_idx, -1)
    return groups, pos.astype(np.int32)


def _batcher_pairs(n):
    pairs = []
    p = 1
    while p < n:
        k = p
        while k >= 1:
            for j in range(k % p, n - k, 2 * k):
                for i in range(min(k, n - j - k)):
                    if (i + j) // (2 * p) == (i + j + k) // (2 * p):
                        pairs.append((i + j, i + j + k))
            k //= 2
        p *= 2
    return pairs


def _top_rows_sorted(s, k, rank=None):
    n, t = s.shape
    if t > LANES:
        parts = [_top_rows_sorted(s[:, c:c + LANES], k, None if rank is None else rank[:, c:c + LANES])
                 for c in range(0, t, LANES)]
        return tuple(jnp.concatenate(z, axis=1) for z in zip(*parts))
    sl = 8
    ng = n // sl
    x = [s[g * sl:(g + 1) * sl] for g in range(ng)]
    if rank is None:
        sub = lax.broadcasted_iota(jnp.int32, (sl, t), 0)
        r = [sub + g * sl for g in range(ng)]
    else:
        r = [rank[g * sl:(g + 1) * sl] for g in range(ng)]

    def before(xa, ra, xb, rb):
        return (xa > xb) | ((xa == xb) & (ra < rb))

    for a, b in [ab for ab in _batcher_pairs(1 << (ng - 1).bit_length()) if ab[1] < ng]:
        keep = before(x[a], r[a], x[b], r[b])
        x[a], x[b] = jnp.where(keep, x[a], x[b]), jnp.where(keep, x[b], x[a])
        r[a], r[b] = jnp.where(keep, r[a], r[b]), jnp.where(keep, r[b], r[a])

    vals, idxs = [], []
    for p in range(k):
        wv, wr = x[0], r[0]
        for shift in (4, 2, 1):
            pv, pr = pltpu.roll(wv, shift, 0), pltpu.roll(wr, shift, 0)
            keep = before(wv, wr, pv, pr) | (wr == pr)
            wv, wr = jnp.where(keep, wv, pv), jnp.where(keep, wr, pr)
        vals.append(wv[0:1])
        idxs.append(wr[0:1])
        popped = r[0] == wr
        for d in range(min(ng, k - p - 1)):
            if d + 1 < ng:
                x[d] = jnp.where(popped, x[d + 1], x[d])
                r[d] = jnp.where(popped, r[d + 1], r[d])
            else:
                x[d] = jnp.where(popped, NEG_INF, x[d])
    return jnp.concatenate(vals, axis=0), jnp.concatenate(idxs, axis=0)


def _peer_select_kernel(xn_ref, wq_t_ref, keys_ref, pos_ref, w_ref,
                        q_ref, v_ref, ix_ref, e_ref, gt_ref, i1_ref, i2_ref, g_ref, *, groups):
    nt = (((1,), (1,)), ((), ()))
    tt = xn_ref.shape[0]

    @pl.when(pl.program_id(0) == 0)
    def _no_previous_tile():
        for ref in (i1_ref, i2_ref, g_ref):
            ref[...] = jnp.zeros(ref.shape, F32)

    q_ref[...] = lax.dot_general(wq_t_ref[...], xn_ref[...], nt,
                                 preferred_element_type=F32).astype(BF16)

    row = lax.broadcasted_iota(jnp.int32, (N_KEYS, N_KEYS), 0)
    pr = row & (W_ROWS - 1)
    half = I1_GROUP // 2
    i1_of_row = ((pr // half) * I1_GROUP + (pr % half) + half * (row // W_ROWS)).astype(F32)
    sub = row.astype(F32)
    hi_mask = jnp.uint32(0xFFFF0000)
    nch, rpc = w_ref.shape[0], W_ROWS // w_ref.shape[0]

    def gate_words(t):
        r1 = i1_ref[pl.ds(t, 1), :]
        r2 = i2_ref[pl.ds(t, 1), :]
        gg = g_ref[pl.ds(t, 1), :]
        lt = jnp.where(i1_of_row == r1, 1.0, 0.0).astype(BF16)
        rt = jnp.where(sub == r2, gg, 0.0).astype(BF16)
        w = lax.dot_general(lt, rt, nt, preferred_element_type=F32)
        bits = lax.bitcast_convert_type(w.astype(BF16).astype(F32), jnp.uint32)
        words = (bits[W_ROWS:] & hi_mask) | (bits[0:W_ROWS] >> 16)
        w_ref[:, pl.ds(pl.multiple_of(t * rpc, rpc), rpc), :] = words.reshape(nch, rpc, N_KEYS)

    per_half = tt // (2 * PEER_HEADS)

    def half_topk(hc, carry):
        qh = q_ref[pl.ds(pl.multiple_of(hc * PEER_HALF, PEER_HALF), PEER_HALF), :]
        s = jnp.dot(keys_ref[hc % 2], qh, preferred_element_type=F32)
        for u in range(per_half):
            gate_words(hc * per_half + u)
        vals, idxs = _top_rows_sorted(s, PEER_TOPK)
        v_ref[hc] = vals
        ix_ref[hc] = idxs
        return carry

    lax.fori_loop(0, 2 * PEER_HEADS, half_topk, 0, unroll=2)

    pos_tab = pos_ref[...]

    def head_select(h, carry):
        v1, v2 = v_ref[2 * h], v_ref[2 * h + 1]
        i1, i2 = ix_ref[2 * h], ix_ref[2 * h + 1]
        v1g, v2g, i1g, i2g = [], [], [], []
        for a, n in groups:
            v1g.append(jnp.broadcast_to(v1[a:a + 1], (n, tt)))
            i1g.append(jnp.broadcast_to(i1[a:a + 1], (n, tt)))
            v2g.append(v2[0:n])
            i2g.append(i2[0:n])
        v1g.append(v1[8:16]); i1g.append(i1[8:16])
        v2g.append(jnp.broadcast_to(v2[0:1], (8, tt))); i2g.append(jnp.broadcast_to(i2[0:1], (8, tt)))
        cand = jnp.concatenate(v1g, axis=0) + jnp.concatenate(v2g, axis=0)
        eidx = jnp.concatenate(i1g, axis=0) * N_KEYS + jnp.concatenate(i2g, axis=0)
        cand = jnp.where(pos_tab >= 0, cand, NEG_INF)
        key = pos_tab * (N_KEYS * N_KEYS) + eidx
        top, sel = _top_rows_sorted(cand, PEER_TOPK, rank=key)
        ek = sel & (N_KEYS * N_KEYS - 1)
        ex = jnp.exp(top - top[0:1])
        gate = ex / jnp.sum(ex, axis=0, keepdims=True)
        e_ref[h] = ek.astype(F32)
        gt_ref[h] = gate
        return carry

    lax.fori_loop(0, PEER_HEADS, head_select, 0, unroll=2)

    ef = e_ref[...].reshape(PEER_HEADS * PEER_TOPK, tt)
    i1f = jnp.floor(ef * (1.0 / N_KEYS))
    i1_ref[...] = i1f.T
    i2_ref[...] = (ef - i1f * N_KEYS).T
    g_ref[...] = gt_ref[...].reshape(PEER_HEADS * PEER_TOPK, tt).T


def _peer_select(xn, wq, keys1, keys2, n1, tt=256):
    t, d = xn.shape
    groups, pos = _cand_tables()
    pos_tab = jnp.broadcast_to(jnp.asarray(pos)[:, None], (pos.shape[0], tt))
    nslot = PEER_HEADS * PEER_TOPK
    qw = 2 * PEER_HEADS * PEER_HALF
    nch = N_KEYS // n1
    rpc = W_ROWS // nch
    const = lambda i: (0, 0)
    kern = functools.partial(_peer_select_kernel, groups=groups)
    ntile = t // tt
    return pl.pallas_call(
        kern,
        grid=(ntile + 1,),
        in_specs=[pl.BlockSpec((tt, d), lambda i: (jnp.minimum(i, ntile - 1), 0)),
                  pl.BlockSpec((qw, d), const),
                  pl.BlockSpec((2, N_KEYS, PEER_HALF), lambda i: (0, 0, 0)),
                  pl.BlockSpec((pos.shape[0], tt), const)],
        out_specs=pl.BlockSpec((nch, tt * rpc, N_KEYS), lambda i: (0, jnp.maximum(i - 1, 0), 0)),
        out_shape=jax.ShapeDtypeStruct((nch, t * rpc, N_KEYS), jnp.uint32),
        scratch_shapes=[pltpu.VMEM((qw, tt), BF16),
                        pltpu.VMEM((2 * PEER_HEADS, PEER_TOPK, tt), F32),
                        pltpu.VMEM((2 * PEER_HEADS, PEER_TOPK, tt), jnp.int32),
                        pltpu.VMEM((PEER_HEADS, PEER_TOPK, tt), F32),
                        pltpu.VMEM((PEER_HEADS, PEER_TOPK, tt), F32),
                        pltpu.VMEM((tt, nslot), F32),
                        pltpu.VMEM((tt, nslot), F32),
                        pltpu.VMEM((tt, nslot), F32)],
        compiler_params=_cparams(("arbitrary",)),
        name="peer_select",
    )(xn, wq.T.astype(BF16), jnp.stack([keys1, keys2]).astype(BF16), pos_tab)


def _gelu(x):
    c = math.sqrt(2.0 / math.pi)
    return 0.5 * x * (1.0 + jnp.tanh(c * (x + 0.044715 * (x * x * x))))


def _peer_experts_kernel(x1_ref, xn_ref, w_ref, u_ref, v_ref, fg_ref, y_ref, acc_ref, *, tt, n1):
    c = pl.program_id(1)
    nc = pl.num_programs(1)
    nt = (((1,), (1,)), ((), ()))
    hi_mask = jnp.uint32(0xFFFF0000)
    half = I1_GROUP // 2
    rpc = n1 // 2

    @pl.when(c == 0)
    def _init():
        acc_ref[...] = jnp.zeros(acc_ref.shape, F32)

    a = lax.dot_general(xn_ref[...], u_ref[...], nt, preferred_element_type=F32)
    lo, hi = [], []
    for g in range(n1 // I1_GROUP):
        for u in range(half):
            words = w_ref[0, pl.ds(g * half + u, tt, stride=rpc), :]
            lo.append(lax.bitcast_convert_type(words << 16, F32))
            hi.append(lax.bitcast_convert_type(words & hi_mask, F32))
    wc = jnp.concatenate(
        [part[g * half + u] for g in range(n1 // I1_GROUP) for part in (lo, hi) for u in range(half)], axis=1)
    z = (_gelu(a) * wc).astype(BF16)
    acc_ref[...] += jnp.dot(z, v_ref[...], preferred_element_type=F32)

    @pl.when(c == nc - 1)
    def _fin():
        y_ref[...] = _rms(x1_ref[...] + acc_ref[...], fg_ref[...])


def _peer_experts(x1, xn, words, u_tab, v_tab, final_g, n1, tt=512):
    t, d = x1.shape
    ne = u_tab.shape[0]
    ec = n1 * N_KEYS
    rpc = n1 // 2
    tok = lambda i, c: (i, 0)
    kern = functools.partial(_peer_experts_kernel, tt=tt, n1=n1)
    return pl.pallas_call(
        kern,
        grid=(t // tt, ne // ec),
        in_specs=[pl.BlockSpec((tt, d), tok), pl.BlockSpec((tt, d), tok),
                  pl.BlockSpec((1, tt * rpc, N_KEYS), lambda i, c: (c, i, 0)),
                  pl.BlockSpec((ec, d), lambda i, c: (c, 0)),
                  pl.BlockSpec((ec, d), lambda i, c: (c, 0)),
                  pl.BlockSpec((1, d), lambda i, c: (0, 0))],
        out_specs=pl.BlockSpec((tt, d), tok),
        out_shape=jax.ShapeDtypeStruct((t, d), F32),
        scratch_shapes=[pltpu.VMEM((tt, d), F32)],
        compiler_params=_cparams(("parallel", "arbitrary")),
        name="peer_experts",
    )(x1, xn, words, u_tab, v_tab, final_g.reshape(1, d))


def _trunk(x, norm1_g, w_in, lam_vecs, subln_g, w_out, norm2_g, wq, keys1, keys2, u_bf, v_bf, final_g):
    b, s, d = x.shape
    qa_t, ka, va_t, qkv = _in_proj(x, norm1_g, w_in)
    oa = _diff_attn(qa_t, ka, va_t, lam_vecs, subln_g)
    obs, lses = [], []
    for (window, dil), (q2, k2, v2) in zip(DIL_PATTERNS, qkv):
        o, lse = _dil_attn(q2, k2, v2, window, dil)
        obs.append(o)
        lses.append(lse)
    x1, xn2 = _out_proj(x, oa, obs, lses, w_out, norm2_g)
    x1, xn2 = x1.reshape(b * s, d), xn2.reshape(b * s, d)
    words = _peer_select(xn2, wq, keys1, keys2, PEER_N1)
    y = _peer_experts(x1, xn2, words, u_bf, v_bf, final_g, PEER_N1)
    return y.reshape(b, s, d)


def kernel(x_prompt, x_sample, norm1_g, w_in, lambda_q1, lambda_k1, lambda_q2, lambda_k2, subln_g, w_out,
           norm2_g, peer_wq, peer_keys1, peer_keys2, peer_u, peer_v, final_g):
    lam_vecs = jnp.concatenate([lambda_q1, lambda_k1, lambda_q2, lambda_k2], axis=0).astype(F32)
    u_bf = peer_u[0].astype(BF16)
    v_bf = peer_v[0].astype(BF16)
    args = (norm1_g[0], w_in[0], lam_vecs, subln_g[0], w_out[0], norm2_g[0], peer_wq[0],
            peer_keys1[0], peer_keys2[0], u_bf, v_bf, final_g)
    return (_trunk(x_prompt, *args), _trunk(x_sample, *args))
```

```python
import functools
import math

import jax
import jax.numpy as jnp
import numpy as np
from jax import lax
from jax.experimental import pallas as pl
from jax.experimental.pallas import tpu as pltpu

D_MODEL = 1024
HEAD_DIM = 64
DIFF_HEADS = 4
DIFF_V_DIM = 2 * HEAD_DIM
DIL_HEADS = 8
DIL_PATTERNS = ((128, 1), (512, 4), (2048, 16))
A_W = DIFF_HEADS * 2 * HEAD_DIM
B_W = DIL_HEADS * HEAD_DIM
PEER_HEADS = 8
N_KEYS = 128
PEER_HALF = 128
PEER_TOPK = 16
EPS = 1e-6
LAMBDA_INIT = 0.8 - 0.6 * math.exp(-0.3 * 0)

LOG2E = math.log2(math.e)
ATT_CHUNK = 256
VT_ROWS = DIFF_V_DIM + 16
LANES = 128
VMEM_LIMIT = 56 * 1024 * 1024

F32 = jnp.float32
BF16 = jnp.bfloat16
NEG_INF = float("-inf")


def _cparams(sem):
    return pltpu.CompilerParams(dimension_semantics=sem, vmem_limit_bytes=VMEM_LIMIT)


def _rms(x, g):
    return x * lax.rsqrt(jnp.mean(x * x, axis=-1, keepdims=True) + EPS) * g


def _in_proj_kernel(x_ref, g_ref, wqa_t_ref, wva_t_ref, wrest_ref, qa_t_ref, ka_ref, va_t_ref, *rest_refs, tkc):
    dil_refs, slab_ref = rest_refs[:-1], rest_refs[-1]
    xn = _rms(x_ref[0], g_ref[...]).astype(BF16)
    tm = xn.shape[0]
    nt = (((1,), (1,)), ((), ()))
    qa_t_ref[0] = lax.dot_general(wqa_t_ref[...], xn, nt, preferred_element_type=F32).astype(BF16)
    va_t = lax.dot_general(wva_t_ref[...], xn, nt, preferred_element_type=F32).astype(BF16)
    ones = jnp.ones((VT_ROWS - DIFF_V_DIM, tkc), BF16)
    for hd in range(DIFF_HEADS):
        for c in range(tm // tkc):
            va_t_ref[0, hd, c, 0:DIFF_V_DIM, :] = va_t[hd * DIFF_V_DIM:(hd + 1) * DIFF_V_DIM, c * tkc:(c + 1) * tkc]
            va_t_ref[0, hd, c, DIFF_V_DIM:VT_ROWS, :] = ones
    rest = jnp.dot(xn, wrest_ref[...], preferred_element_type=F32)
    ka_ref[0] = rest[:, 0:A_W].astype(BF16)
    nsl = B_W // LANES
    for a in range(3):
        cols = rest[:, (1 + a) * B_W:(2 + a) * B_W]
        dil_refs[a][0] = cols.astype(BF16)
        for sl in range(nsl):
            slab_ref[a * nsl + sl] = cols[:, sl * LANES:(sl + 1) * LANES]
    for pi, (_, dil) in enumerate(DIL_PATTERNS[1:]):
        for a in range(3):
            out = dil_refs[3 * (pi + 1) + a]
            for r in range(dil):
                for sl in range(nsl):
                    rows = slab_ref[a * nsl + sl, pl.ds(r, tm // dil, stride=dil), :]
                    out[0, :, r * B_W + sl * LANES:r * B_W + (sl + 1) * LANES] = rows.astype(BF16)


def _in_proj(x, g, w_in, tm=512, tkc=ATT_CHUNK):
    b, s, d = x.shape
    assert DIL_PATTERNS[0][1] == 1
    scale = HEAD_DIM ** -0.5
    wqa_t = (w_in[:, 0:A_W] * (scale * LOG2E)).T.astype(BF16)
    wva_t = w_in[:, 2 * A_W:3 * A_W].T.astype(BF16)
    wrest = jnp.concatenate(
        [w_in[:, A_W:2 * A_W], w_in[:, 3 * A_W:4 * A_W] * (scale * LOG2E), w_in[:, 4 * A_W:]], axis=1).astype(BF16)
    tok = lambda i, j: (i, j, 0)
    tr = lambda i, j: (i, 0, j)
    const = lambda i, j: (0, 0)
    dil_specs, dil_shapes = [], []
    for _, dil in DIL_PATTERNS:
        dil_specs += [pl.BlockSpec((1, tm // dil, dil * B_W), tok)] * 3
        dil_shapes += [jax.ShapeDtypeStruct((b, s // dil, dil * B_W), BF16)] * 3
    outs = pl.pallas_call(
        functools.partial(_in_proj_kernel, tkc=tkc),
        grid=(b, s // tm),
        in_specs=[pl.BlockSpec((1, tm, d), tok),
                  pl.BlockSpec((1, d), const),
                  pl.BlockSpec((A_W, d), const),
                  pl.BlockSpec((A_W, d), const),
                  pl.BlockSpec((d, 4 * A_W), const)],
        out_specs=[pl.BlockSpec((1, A_W, tm), tr),
                   pl.BlockSpec((1, tm, A_W), tok),
                   pl.BlockSpec((1, DIFF_HEADS, tm // tkc, VT_ROWS, tkc), lambda i, j: (i, 0, j, 0, 0))] + dil_specs,
        out_shape=[jax.ShapeDtypeStruct((b, A_W, s), BF16),
                   jax.ShapeDtypeStruct((b, s, A_W), BF16),
                   jax.ShapeDtypeStruct((b, DIFF_HEADS, s // tkc, VT_ROWS, tkc), BF16)] + dil_shapes,
        scratch_shapes=[pltpu.VMEM((3 * B_W // LANES, tm, LANES), F32)],
        compiler_params=_cparams(("parallel", "parallel")),
        name="in_proj",
    )(x, g.reshape(1, d), wqa_t, wva_t, wrest)
    qkv = [tuple(outs[3 + 3 * p:6 + 3 * p]) for p in range(len(DIL_PATTERNS))]
    return outs[0], outs[1], outs[2], qkv


def _bf16_split3(x):
    x = np.asarray(x, np.float32)
    pieces = []
    for _ in range(3):
        p = x.astype(jnp.bfloat16).astype(np.float32)
        pieces.append(p.astype(np.float64))
        x = (x - p).astype(np.float32)
    return pieces


def _alibi_tables(tq, tkc):
    slopes = 2.0 ** (-8.0 * np.arange(1, DIFF_HEADS + 1) / DIFF_HEADS)
    a = (slopes * LOG2E).astype(np.float32)
    kx = np.zeros((tkc, LANES), np.float32)
    s_rel = np.arange(tkc, dtype=np.float32)
    s_lo = s_rel % 256
    kx[:, 0:3] = s_lo[:, None]
    kx[:, 3:6] = (s_rel - s_lo)[:, None]
    kx[:, 6:9] = 1.0
    qx = np.zeros((DIFF_HEADS, 2, LANES, tq), np.float64)
    t_rel = np.arange(tq, dtype=np.float32)
    for hd in range(DIFF_HEADS):
        a3 = _bf16_split3(a[hd])
        f3 = _bf16_split3(-(a[hd] * t_rel).astype(np.float32))
        for r in range(3):
            qx[hd, 0, r, :] = a3[r]
            qx[hd, 0, 3 + r, :] = a3[r]
            qx[hd, 0, 6 + r, :] = f3[r]
    qx[:, 1] = -qx[:, 0]
    rel = s_rel[:, None] - t_rel[None, :]
    ndiag = tq // tkc
    diag = np.stack([np.stack([-(a[hd] * np.abs(rel + d * tkc)) for d in range(ndiag)])
                     for hd in range(DIFF_HEADS)]).astype(np.float32)
    return (jnp.asarray(a), jnp.asarray(kx, F32).astype(BF16), jnp.asarray(qx, F32).astype(BF16),
            jnp.asarray(diag))


def _diff_attn_kernel(a_ref, qt_ref, k_ref, vt_ref, kx_ref, qx_ref, diag_ref, lam_ref, sg_ref, o_ref,
                      qs_ref, m_ref, acc_ref, s_ref, *, tq, tkc, unroll):
    h = pl.program_id(1)
    i = pl.program_id(2)
    nk = k_ref.shape[1] // tkc
    ndiag = tq // tkc
    a_h = a_ref[h]

    qt = qt_ref[0]
    zero = jnp.zeros((HEAD_DIM, tq), BF16)
    for side in range(2):
        qs_ref[side, 0:HEAD_DIM, 0:tq] = qt[0:HEAD_DIM]
        qs_ref[side, 0:HEAD_DIM, tq:2 * tq] = zero
        qs_ref[side, HEAD_DIM:DIFF_V_DIM, 0:tq] = zero
        qs_ref[side, HEAD_DIM:DIFF_V_DIM, tq:2 * tq] = qt[HEAD_DIM:]
        qs_ref[side, DIFF_V_DIM:, 0:tq] = qx_ref[0, side]
        qs_ref[side, DIFF_V_DIM:, tq:2 * tq] = qx_ref[0, side]
    m_ref[...] = jnp.full(m_ref.shape, NEG_INF, F32)
    acc_ref[...] = jnp.zeros(acc_ref.shape, F32)

    def update(j, s, c):
        m_old = m_ref[...]
        m_new = jnp.maximum(m_old, jnp.max(s, axis=0, keepdims=True) + c)
        alpha = jnp.exp2(m_old - m_new)
        p = jnp.exp2(s - (m_new - c)).astype(BF16)
        acc_ref[...] = alpha * acc_ref[...] + jnp.dot(vt_ref[0, 0, j], p, preferred_element_type=F32)
        m_ref[...] = m_new

    n_left = i * ndiag

    def scores_diag(d, slot):
        k0 = pl.multiple_of((n_left + d) * tkc, tkc)
        s_ref[slot] = jnp.dot(k_ref[0, pl.ds(k0, tkc), :], qs_ref[0, 0:DIFF_V_DIM, :],
                              preferred_element_type=F32)

    def consume_diag(d, slot):
        bias = diag_ref[0, d]
        update(n_left + d, s_ref[slot] + jnp.concatenate([bias, bias], axis=1), jnp.float32(0.0))

    cnt = nk - ndiag
    last = cnt - 1

    def chunk_of(n):
        n = jnp.minimum(n, last)
        side = (n >= n_left).astype(jnp.int32)
        j = n + side * ndiag
        dist = (1 - 2 * side) * (i * tq - j * tkc)
        return j, side, -a_h * dist.astype(F32)

    def scores(n, slot):
        j, side, _ = chunk_of(n)
        k0 = pl.multiple_of(j * tkc, tkc)
        lhs = jnp.concatenate([k_ref[0, pl.ds(k0, tkc), :], kx_ref[...]], axis=1)
        s_ref[slot] = jnp.dot(lhs, qs_ref[side], preferred_element_type=F32)

    def consume(n, slot):
        j, _, c = chunk_of(n)
        update(j, s_ref[slot], c)

    npeel = cnt % unroll
    head = [(scores_diag, consume_diag, d) for d in range(ndiag)] + [(scores, consume, n) for n in range(npeel)]
    assert len(head) % 2 == 0 and unroll % 2 == 0
    head[0][0](head[0][2], 0)
    for t, (_, cons, arg) in enumerate(head):
        if t + 1 < len(head):
            head[t + 1][0](head[t + 1][2], (t + 1) % 2)
        elif npeel < cnt:
            scores(npeel, 0)
        cons(arg, t % 2)

    def group(g, carry):
        for u in range(unroll):
            scores(npeel + g * unroll + u + 1, (u + 1) % 2)
            consume(npeel + g * unroll + u, u % 2)
        return carry

    lax.fori_loop(0, (cnt - npeel) // unroll, group, 0)

    lam_v = lam_ref[...]
    lam = (jnp.exp(jnp.sum(lam_v[0:1] * lam_v[1:2], axis=1, keepdims=True))
           - jnp.exp(jnp.sum(lam_v[2:3] * lam_v[3:4], axis=1, keepdims=True)) + LAMBDA_INIT)
    acc = acc_ref[...]
    o = acc[0:DIFF_V_DIM] / acc[DIFF_V_DIM:DIFF_V_DIM + 1]
    o = o[:, 0:tq] - lam * o[:, tq:2 * tq]
    ms = jnp.mean(o * o, axis=0, keepdims=True)
    o = o * lax.rsqrt(ms + EPS) * sg_ref[...] * (1.0 - LAMBDA_INIT)
    o_ref[0] = o.T.astype(BF16)


def _diff_attn(qa_t, ka, va_t, lam_vecs, subln_g, tq=512, tkc=ATT_CHUNK, unroll=14):
    b, _, s = qa_t.shape
    nk = s // tkc
    a, kx, qx, diag = _alibi_tables(tq, tkc)
    kern = functools.partial(_diff_attn_kernel, tq=tq, tkc=tkc, unroll=unroll)
    c2 = lambda b_, h, i: (0, 0)
    return pl.pallas_call(
        kern,
        grid=(b, DIFF_HEADS, s // tq),
        in_specs=[pl.BlockSpec(memory_space=pltpu.SMEM),
                  pl.BlockSpec((1, DIFF_V_DIM, tq), lambda b_, h, i: (b_, h, i)),
                  pl.BlockSpec((1, s, DIFF_V_DIM), lambda b_, h, i: (b_, 0, h)),
                  pl.BlockSpec((1, 1, nk, VT_ROWS, tkc), lambda b_, h, i: (b_, h, 0, 0, 0)),
                  pl.BlockSpec((tkc, LANES), c2),
                  pl.BlockSpec((1, 2, LANES, tq), lambda b_, h, i: (h, 0, 0, 0)),
                  pl.BlockSpec((1, tq // tkc, tkc, tq), lambda b_, h, i: (h, 0, 0, 0)),
                  pl.BlockSpec((4, HEAD_DIM), c2),
                  pl.BlockSpec((DIFF_V_DIM, 1), c2)],
        out_specs=pl.BlockSpec((1, tq, DIFF_V_DIM), lambda b_, h, i: (b_, i, h)),
        out_shape=jax.ShapeDtypeStruct((b, s, A_W), BF16),
        scratch_shapes=[pltpu.VMEM((2, 2 * LANES, 2 * tq), BF16),
                        pltpu.VMEM((1, 2 * tq), F32),
                        pltpu.VMEM((VT_ROWS, 2 * tq), F32),
                        pltpu.VMEM((2, tkc, 2 * tq), F32)],
        compiler_params=_cparams(("parallel", "parallel", "arbitrary")),
        name="diff_attn",
    )(a, qa_t, ka, va_t, kx, qx, diag, lam_vecs, subln_g.reshape(DIFF_V_DIM, 1))


def _dil_attn_kernel(bias_ref, q_ref, kp_ref, kc_ref, kn_ref, vp_ref, vc_ref, vn_ref,
                     o_ref, lse_ref, s_ref, *, n_sub, blk, qt):
    i = pl.program_id(2)
    nsb = qt // blk
    col = lax.broadcasted_iota(jnp.int32, (1, 3 * blk), 1)
    lane = lax.broadcasted_iota(jnp.int32, (blk, LANES), 1)
    first = lane < HEAD_DIM
    nt = (((1,), (1,)), ((), ()))

    def window(sb, prev_ref, cur_ref, next_ref, cols):
        parts = [prev_ref[0, :, cols] if sb == 0 else cur_ref[0, (sb - 1) * blk:sb * blk, cols],
                 cur_ref[0, sb * blk:(sb + 1) * blk, cols],
                 next_ref[0, :, cols] if sb == nsb - 1 else cur_ref[0, (sb + 1) * blk:(sb + 2) * blk, cols]]
        return jnp.concatenate(parts, axis=0)

    def scores(sb, hp, slot):
        cols = slice(hp * LANES, (hp + 1) * LANES)
        q = q_ref[0, sb * blk:(sb + 1) * blk, cols]
        zq = jnp.zeros_like(q)
        q2 = jnp.concatenate([jnp.where(first, q, zq), jnp.where(first, zq, q)], axis=0)
        s_ref[slot] = lax.dot_general(q2, window(sb, kp_ref, kc_ref, kn_ref, cols), nt,
                                      preferred_element_type=F32)

    def finish(sb, hp, slot):
        cols = slice(hp * LANES, (hp + 1) * LANES)
        s = s_ref[slot] + jnp.concatenate([bias_ref[2 * hp], bias_ref[2 * hp + 1]], axis=0)
        if sb == 0 or sb == nsb - 1:
            kidx = i * qt + (sb - 1) * blk + col
            s = jnp.where((kidx >= 0) & (kidx < n_sub), s, NEG_INF)
        m = jnp.max(s, axis=1, keepdims=True)
        p = jnp.exp2(s - m)
        l = jnp.sum(p, axis=1, keepdims=True)
        o = jnp.dot(p.astype(BF16), window(sb, vp_ref, vc_ref, vn_ref, cols), preferred_element_type=F32) / l
        lse = jnp.broadcast_to(m + jnp.log2(l), (2 * blk, LANES))
        o_ref[0, sb * blk:(sb + 1) * blk, cols] = jnp.where(first, o[0:blk], o[blk:]).astype(BF16)
        lse_ref[0, sb * blk:(sb + 1) * blk, cols] = jnp.where(first, lse[0:blk], lse[blk:])

    units = [(sb, hp) for sb in range(nsb) for hp in range(DIL_HEADS // 2)]
    nslot = s_ref.shape[0]
    for n in range(min(nslot - 1, len(units))):
        scores(*units[n], n % nslot)
    for n, unit in enumerate(units):
        if n + nslot - 1 < len(units):
            scores(*units[n + nslot - 1], (n + nslot - 1) % nslot)
        finish(*unit, n % nslot)


def _dil_attn(q2, k2, v2, window, dil, blk=128, qt=512):
    shape2 = q2.shape
    b, n_sub, w = shape2[0], shape2[1], shape2[2] // dil
    qt = min(qt, n_sub)
    nq = n_sub // qt
    nblk = n_sub // blk
    half = window // (2 * dil)
    slopes = 2.0 ** (-8.0 * np.arange(1, DIL_HEADS + 1) / DIL_HEADS)
    joff = np.arange(3 * blk)[None, :] - blk - np.arange(blk)[:, None]
    band = np.abs(joff) <= half
    bias = np.where(band[None], -(slopes * LOG2E)[:, None, None] * (np.abs(joff) * dil)[None],
                    -np.inf).astype(np.float32)
    cur = lambda b_, r, i: (b_, i, r)
    prv = lambda b_, r, i: (b_, jnp.maximum(i * (qt // blk) - 1, 0), r)
    nxt = lambda b_, r, i: (b_, jnp.minimum((i + 1) * (qt // blk), nblk - 1), r)
    big = lambda f: pl.BlockSpec((1, qt, w), f)
    halo = lambda f: pl.BlockSpec((1, blk, w), f)
    kern = functools.partial(_dil_attn_kernel, n_sub=n_sub, blk=blk, qt=qt)
    o, lse = pl.pallas_call(
        kern,
        grid=(b, dil, nq),
        in_specs=[pl.BlockSpec((DIL_HEADS, blk, 3 * blk), lambda b_, r, i: (0, 0, 0)),
                  big(cur), halo(prv), big(cur), halo(nxt), halo(prv), big(cur), halo(nxt)],
        out_specs=[big(cur), big(cur)],
        out_shape=[jax.ShapeDtypeStruct(shape2, BF16), jax.ShapeDtypeStruct(shape2, F32)],
        scratch_shapes=[pltpu.VMEM((3, 2 * blk, 3 * blk), F32)],
        compiler_params=_cparams(("parallel", "parallel", "parallel")),
        name=f"dil_attn_d{dil}",
    )(jnp.asarray(bias), q2, k2, k2, k2, v2, v2, v2)
    return o, lse


def _out_proj_kernel(x_ref, oa_ref, o1_ref, o2_ref, o3_ref, l1_ref, l2_ref, l3_ref,
                     wa_ref, wb_ref, g_ref, x1_ref, xn_ref, slab_ref):
    tm = x_ref.shape[1]
    nsl = B_W // LANES

    def natural(ref, dil, base):
        if dil == 1:
            return ref[0].astype(F32)
        for r in range(dil):
            for sl in range(nsl):
                slab_ref[base + sl, pl.ds(r, tm // dil, stride=dil), :] = (
                    ref[0, :, r * B_W + sl * LANES:r * B_W + (sl + 1) * LANES].astype(F32))
        return jnp.concatenate([slab_ref[base + sl] for sl in range(nsl)], axis=1)

    dils = [dil for _, dil in DIL_PATTERNS]
    l1, l2, l3 = [natural(ref, dil, (2 * n) * nsl) for n, (ref, dil) in enumerate(zip((l1_ref, l2_ref, l3_ref), dils))]
    o1, o2, o3 = [natural(ref, dil, (2 * n + 1) * nsl) for n, (ref, dil) in enumerate(zip((o1_ref, o2_ref, o3_ref), dils))]
    m = jnp.maximum(jnp.maximum(l1, l2), l3)
    e1, e2, e3 = jnp.exp2(l1 - m), jnp.exp2(l2 - m), jnp.exp2(l3 - m)
    ob = (e1 * o1 + e2 * o2 + e3 * o3) / (e1 + e2 + e3)
    y = (x_ref[0]
         + jnp.dot(oa_ref[0], wa_ref[...], preferred_element_type=F32)
         + jnp.dot(ob.astype(BF16), wb_ref[...], preferred_element_type=F32))
    x1_ref[0] = y
    xn_ref[0] = _rms(y, g_ref[...]).astype(BF16)


def _out_proj(x, oa, obs, lses, w_out, g, tm=512):
    b, s, d = x.shape
    wa = w_out[0:A_W].astype(BF16)
    wb = w_out[A_W:].astype(BF16)
    tok = lambda i, j: (i, j, 0)
    const = lambda i, j: (0, 0)
    dil_specs = [pl.BlockSpec((1, tm // dil, dil * B_W), tok) for _, dil in DIL_PATTERNS]
    return pl.pallas_call(
        _out_proj_kernel,
        grid=(b, s // tm),
        in_specs=[pl.BlockSpec((1, tm, d), tok), pl.BlockSpec((1, tm, A_W), tok)] + dil_specs + dil_specs +
                 [pl.BlockSpec((A_W, d), const), pl.BlockSpec((B_W, d), const), pl.BlockSpec((1, d), const)],
        out_specs=[pl.BlockSpec((1, tm, d), tok), pl.BlockSpec((1, tm, d), tok)],
        out_shape=[jax.ShapeDtypeStruct((b, s, d), F32), jax.ShapeDtypeStruct((b, s, d), BF16)],
        scratch_shapes=[pltpu.VMEM((2 * len(DIL_PATTERNS) * B_W // LANES, tm, LANES), F32)],
        compiler_params=_cparams(("parallel", "parallel")),
        name="out_proj",
    )(x, oa, *obs, *lses, wa, wb, g.reshape(1, d))


W_ROWS = N_KEYS // 2
I1_GROUP = 8
PEER_N1 = 16


def _cand_tables():
    groups = [(0, 16), (1, 8)] + [(a, 8) for a in range(2, 8)]
    a_idx, b_idx = [], []
    for a, n in groups:
        a_idx += [a] * n
        b_idx += list(range(n))
    a_idx += list(range(8, 16))
    b_idx += [0] * 8
    a_idx, b_idx = np.array(a_idx), np.array(b_idx)
    ok = (a_idx + 1) * (b_idx + 1) <= PEER_TOPK
    pos = np.where(ok, a_idx * PEER_TOPK + b_idx, -1)
    return groups, pos.astype(np.int32)


def _batcher_pairs(n):
    pairs = []
    p = 1
    while p < n:
        k = p
        while k >= 1:
            for j in range(k % p, n - k, 2 * k):
                for i in range(min(k, n - j - k)):
                    if (i + j) // (2 * p) == (i + j + k) // (2 * p):
                        pairs.append((i + j, i + j + k))
            k //= 2
        p *= 2
    return pairs


def _top_rows_sorted(s, k, rank=None):
    n, t = s.shape
    if t > LANES:
        parts = [_top_rows_sorted(s[:, c:c + LANES], k, None if rank is None else rank[:, c:c + LANES])
                 for c in range(0, t, LANES)]
        return tuple(jnp.concatenate(z, axis=1) for z in zip(*parts))
    sl = 8
    ng = n // sl
    x = [s[g * sl:(g + 1) * sl] for g in range(ng)]
    if rank is None:
        sub = lax.broadcasted_iota(jnp.int32, (sl, t), 0)
        r = [sub + g * sl for g in range(ng)]
    else:
        r = [rank[g * sl:(g + 1) * sl] for g in range(ng)]

    def before(xa, ra, xb, rb):
        return (xa > xb) | ((xa == xb) & (ra < rb))

    for a, b in [ab for ab in _batcher_pairs(1 << (ng - 1).bit_length()) if ab[1] < ng]:
        keep = before(x[a], r[a], x[b], r[b])
        x[a], x[b] = jnp.where(keep, x[a], x[b]), jnp.where(keep, x[b], x[a])
        r[a], r[b] = jnp.where(keep, r[a], r[b]), jnp.where(keep, r[b], r[a])

    vals, idxs = [], []
    for p in range(k):
        wv, wr = x[0], r[0]
        for shift in (4, 2, 1):
            pv, pr = pltpu.roll(wv, shift, 0), pltpu.roll(wr, shift, 0)
            keep = before(wv, wr, pv, pr) | (wr == pr)
            wv, wr = jnp.where(keep, wv, pv), jnp.where(keep, wr, pr)
        vals.append(wv[0:1])
        idxs.append(wr[0:1])
        popped = r[0] == wr
        for d in range(min(ng, k - p - 1)):
            if d + 1 < ng:
                x[d] = jnp.where(popped, x[d + 1], x[d])
                r[d] = jnp.where(popped, r[d + 1], r[d])
            else:
                x[d] = jnp.where(popped, NEG_INF, x[d])
    return jnp.concatenate(vals, axis=0), jnp.concatenate(idxs, axis=0)


def _peer_select_kernel(xn_ref, wq_t_ref, keys_ref, pos_ref, w_ref,
                        q_ref, v_ref, ix_ref, e_ref, gt_ref, i1_ref, i2_ref, g_ref, *, groups):
    nt = (((1,), (1,)), ((), ()))
    tt = xn_ref.shape[0]

    @pl.when(pl.program_id(0) == 0)
    def _no_previous_tile():
        for ref in (i1_ref, i2_ref, g_ref):
            ref[...] = jnp.zeros(ref.shape, F32)

    q_ref[...] = lax.dot_general(wq_t_ref[...], xn_ref[...], nt,
                                 preferred_element_type=F32).astype(BF16)

    row = lax.broadcasted_iota(jnp.int32, (N_KEYS, N_KEYS), 0)
    pr = row & (W_ROWS - 1)
    half = I1_GROUP // 2
    i1_of_row = ((pr // half) * I1_GROUP + (pr % half) + half * (row // W_ROWS)).astype(F32)
    sub = row.astype(F32)
    hi_mask = jnp.uint32(0xFFFF0000)
    nch, rpc = w_ref.shape[0], W_ROWS // w_ref.shape[0]

    def gate_words(t):
        r1 = i1_ref[pl.ds(t, 1), :]
        r2 = i2_ref[pl.ds(t, 1), :]
        gg = g_ref[pl.ds(t, 1), :]
        lt = jnp.where(i1_of_row == r1, 1.0, 0.0).astype(BF16)
        rt = jnp.where(sub == r2, gg, 0.0).astype(BF16)
        w = lax.dot_general(lt, rt, nt, preferred_element_type=F32)
        bits = lax.bitcast_convert_type(w.astype(BF16).astype(F32), jnp.uint32)
        words = (bits[W_ROWS:] & hi_mask) | (bits[0:W_ROWS] >> 16)
        w_ref[:, pl.ds(pl.multiple_of(t * rpc, rpc), rpc), :] = words.reshape(nch, rpc, N_KEYS)

    per_half = tt // (2 * PEER_HEADS)

    def half_topk(hc, carry):
        qh = q_ref[pl.ds(pl.multiple_of(hc * PEER_HALF, PEER_HALF), PEER_HALF), :]
        s = jnp.dot(keys_ref[hc % 2], qh, preferred_element_type=F32)
        for u in range(per_half):
            gate_words(hc * per_half + u)
        vals, idxs = _top_rows_sorted(s, PEER_TOPK)
        v_ref[hc] = vals
        ix_ref[hc] = idxs
        return carry

    lax.fori_loop(0, 2 * PEER_HEADS, half_topk, 0, unroll=2)

    pos_tab = pos_ref[...]

    def head_select(h, carry):
        v1, v2 = v_ref[2 * h], v_ref[2 * h + 1]
        i1, i2 = ix_ref[2 * h], ix_ref[2 * h + 1]
        v1g, v2g, i1g, i2g = [], [], [], []
        for a, n in groups:
            v1g.append(jnp.broadcast_to(v1[a:a + 1], (n, tt)))
            i1g.append(jnp.broadcast_to(i1[a:a + 1], (n, tt)))
            v2g.append(v2[0:n])
            i2g.append(i2[0:n])
        v1g.append(v1[8:16]); i1g.append(i1[8:16])
        v2g.append(jnp.broadcast_to(v2[0:1], (8, tt))); i2g.append(jnp.broadcast_to(i2[0:1], (8, tt)))
        cand = jnp.concatenate(v1g, axis=0) + jnp.concatenate(v2g, axis=0)
        eidx = jnp.concatenate(i1g, axis=0) * N_KEYS + jnp.concatenate(i2g, axis=0)
        cand = jnp.where(pos_tab >= 0, cand, NEG_INF)
        key = pos_tab * (N_KEYS * N_KEYS) + eidx
        top, sel = _top_rows_sorted(cand, PEER_TOPK, rank=key)
        ek = sel & (N_KEYS * N_KEYS - 1)
        ex = jnp.exp(top - top[0:1])
        gate = ex / jnp.sum(ex, axis=0, keepdims=True)
        e_ref[h] = ek.astype(F32)
        gt_ref[h] = gate
        return carry

    lax.fori_loop(0, PEER_HEADS, head_select, 0, unroll=2)

    ef = e_ref[...].reshape(PEER_HEADS * PEER_TOPK, tt)
    i1f = jnp.floor(ef * (1.0 / N_KEYS))
    i1_ref[...] = i1f.T
    i2_ref[...] = (ef - i1f * N_KEYS).T
    g_ref[...] = gt_ref[...].reshape(PEER_HEADS * PEER_TOPK, tt).T


def _peer_select(xn, wq, keys1, keys2, n1, tt=256):
    t, d = xn.shape
    groups, pos = _cand_tables()
    pos_tab = jnp.broadcast_to(jnp.asarray(pos)[:, None], (pos.shape[0], tt))
    nslot = PEER_HEADS * PEER_TOPK
    qw = 2 * PEER_HEADS * PEER_HALF
    nch = N_KEYS // n1
    rpc = W_ROWS // nch
    const = lambda i: (0, 0)
    kern = functools.partial(_peer_select_kernel, groups=groups)
    ntile = t // tt
    return pl.pallas_call(
        kern,
        grid=(ntile + 1,),
        in_specs=[pl.BlockSpec((tt, d), lambda i: (jnp.minimum(i, ntile - 1), 0)),
                  pl.BlockSpec((qw, d), const),
                  pl.BlockSpec((2, N_KEYS, PEER_HALF), lambda i: (0, 0, 0)),
                  pl.BlockSpec((pos.shape[0], tt), const)],
        out_specs=pl.BlockSpec((nch, tt * rpc, N_KEYS), lambda i: (0, jnp.maximum(i - 1, 0), 0)),
        out_shape=jax.ShapeDtypeStruct((nch, t * rpc, N_KEYS), jnp.uint32),
        scratch_shapes=[pltpu.VMEM((qw, tt), BF16),
                        pltpu.VMEM((2 * PEER_HEADS, PEER_TOPK, tt), F32),
                        pltpu.VMEM((2 * PEER_HEADS, PEER_TOPK, tt), jnp.int32),
                        pltpu.VMEM((PEER_HEADS, PEER_TOPK, tt), F32),
                        pltpu.VMEM((PEER_HEADS, PEER_TOPK, tt), F32),
                        pltpu.VMEM((tt, nslot), F32),
                        pltpu.VMEM((tt, nslot), F32),
                        pltpu.VMEM((tt, nslot), F32)],
        compiler_params=_cparams(("arbitrary",)),
        name="peer_select",
    )(xn, wq.T.astype(BF16), jnp.stack([keys1, keys2]).astype(BF16), pos_tab)


def _gelu(x):
    c = math.sqrt(2.0 / math.pi)
    return 0.5 * x * (1.0 + jnp.tanh(c * (x + 0.044715 * (x * x * x))))


def _peer_experts_kernel(x1_ref, xn_ref, w_ref, u_ref, v_ref, fg_ref, y_ref, acc_ref, *, tt, n1):
    c = pl.program_id(1)
    nc = pl.num_programs(1)
    nt = (((1,), (1,)), ((), ()))
    hi_mask = jnp.uint32(0xFFFF0000)
    half = I1_GROUP // 2
    rpc = n1 // 2

    @pl.when(c == 0)
    def _init():
        acc_ref[...] = jnp.zeros(acc_ref.shape, F32)

    a = lax.dot_general(xn_ref[...], u_ref[...], nt, preferred_element_type=F32)
    lo, hi = [], []
    for g in range(n1 // I1_GROUP):
        for u in range(half):
            words = w_ref[0, pl.ds(g * half + u, tt, stride=rpc), :]
            lo.append(lax.bitcast_convert_type(words << 16, F32))
            hi.append(lax.bitcast_convert_type(words & hi_mask, F32))
    wc = jnp.concatenate(
        [part[g * half + u] for g in range(n1 // I1_GROUP) for part in (lo, hi) for u in range(half)], axis=1)
    z = (_gelu(a) * wc).astype(BF16)
    acc_ref[...] += jnp.dot(z, v_ref[...], preferred_element_type=F32)

    @pl.when(c == nc - 1)
    def _fin():
        y_ref[...] = _rms(x1_ref[...] + acc_ref[...], fg_ref[...])


def _peer_experts(x1, xn, words, u_tab, v_tab, final_g, n1, tt=512):
    t, d = x1.shape
    ne = u_tab.shape[0]
    ec = n1 * N_KEYS
    rpc = n1 // 2
    tok = lambda i, c: (i, 0)
    kern = functools.partial(_peer_experts_kernel, tt=tt, n1=n1)
    return pl.pallas_call(
        kern,
        grid=(t // tt, ne // ec),
        in_specs=[pl.BlockSpec((tt, d), tok), pl.BlockSpec((tt, d), tok),
                  pl.BlockSpec((1, tt * rpc, N_KEYS), lambda i, c: (c, i, 0)),
                  pl.BlockSpec((ec, d), lambda i, c: (c, 0)),
                  pl.BlockSpec((ec, d), lambda i, c: (c, 0)),
                  pl.BlockSpec((1, d), lambda i, c: (0, 0))],
        out_specs=pl.BlockSpec((tt, d), tok),
        out_shape=jax.ShapeDtypeStruct((t, d), F32),
        scratch_shapes=[pltpu.VMEM((tt, d), F32)],
        compiler_params=_cparams(("parallel", "arbitrary")),
        name="peer_experts",
    )(x1, xn, words, u_tab, v_tab, final_g.reshape(1, d))


def _trunk(x, norm1_g, w_in, lam_vecs, subln_g, w_out, norm2_g, wq, keys1, keys2, u_bf, v_bf, final_g):
    b, s, d = x.shape
    qa_t, ka, va_t, qkv = _in_proj(x, norm1_g, w_in)
    oa = _diff_attn(qa_t, ka, va_t, lam_vecs, subln_g)
    obs, lses = [], []
    for (window, dil), (q2, k2, v2) in zip(DIL_PATTERNS, qkv):
        o, lse = _dil_attn(q2, k2, v2, window, dil)
        obs.append(o)
        lses.append(lse)
    x1, xn2 = _out_proj(x, oa, obs, lses, w_out, norm2_g)
    x1, xn2 = x1.reshape(b * s, d), xn2.reshape(b * s, d)
    words = _peer_select(xn2, wq, keys1, keys2, PEER_N1)
    y = _peer_experts(x1, xn2, words, u_bf, v_bf, final_g, PEER_N1)
    return y.reshape(b, s, d)


def kernel(x_prompt, x_sample, norm1_g, w_in, lambda_q1, lambda_k1, lambda_q2, lambda_k2, subln_g, w_out,
           norm2_g, peer_wq, peer_keys1, peer_keys2, peer_u, peer_v, final_g):
    lam_vecs = jnp.concatenate([lambda_q1, lambda_k1, lambda_q2, lambda_k2], axis=0).astype(F32)
    u_bf = peer_u[0].astype(BF16)
    v_bf = peer_v[0].astype(BF16)
    args = (norm1_g[0], w_in[0], lam_vecs, subln_g[0], w_out[0], norm2_g[0], peer_wq[0],
            peer_keys1[0], peer_keys2[0], u_bf, v_bf, final_g)
    return (_trunk(x_prompt, *args), _trunk(x_sample, *args))
```
